```python
import math
import jax, jax.numpy as jnp
from jax import lax
import numpy as np

D_MODEL = 1024
BATCH = 2
SEQ = 8192
DEPTH = 1

MIX_WIDTH = D_MODEL
POOL_WIDTH = MIX_WIDTH // 2
POOL_WINDOWS = (2, 4, 8, 16)
POOL_GROUPS = len(POOL_WINDOWS)
POOL_CH = POOL_WIDTH // POOL_GROUPS
ATTN_WIDTH = MIX_WIDTH - POOL_WIDTH
HEAD_DIM = 64
ATTN_HEADS = ATTN_WIDTH // HEAD_DIM
IN_COLS = POOL_WIDTH + 3 * ATTN_WIDTH
BLOCK = 256
TOP_K_BLOCKS = 3
QCHUNK = 64
NUM_BUCKETS = 32
MAX_DISTANCE = 1024
D_FF = 4 * D_MODEL
PLE_DIM = 256
EPS = 1e-6
NEG = -1e30

kernel_name = "hybrid_pool_moba_layer"


def rmsnorm(x, g):
    xf = x.astype(jnp.float32)
    y = xf * lax.rsqrt(jnp.mean(xf * xf, axis=-1, keepdims=True) + EPS)
    return (y * g.astype(jnp.float32)).astype(x.dtype)


def t5_bucket(rel):
    n = jnp.maximum(rel, 0)
    max_exact = NUM_BUCKETS // 2
    nf = jnp.maximum(n, 1).astype(jnp.float32)
    large = max_exact + (jnp.log(nf / max_exact) / math.log(MAX_DISTANCE / max_exact)
                         * (NUM_BUCKETS - max_exact)).astype(jnp.int32)
    large = jnp.minimum(large, NUM_BUCKETS - 1)
    return jnp.where(n < max_exact, n, large)


def pool_mixer(u, w_pool, scale):
    B, S, _ = u.shape
    ug = u.astype(jnp.float32).reshape(B, S, POOL_GROUPS, POOL_CH)
    c = jnp.pad(jnp.cumsum(ug, axis=1), ((0, 0), (1, 0), (0, 0), (0, 0)))
    t = jnp.arange(S)
    outs = []
    for g, w in enumerate(POOL_WINDOWS):
        cg = c[:, :, g]
        lagged = jnp.pad(cg, ((0, 0), (w, 0), (0, 0)))[:, 1:S + 1]
        cnt = jnp.minimum(t + 1, w).astype(jnp.float32)[None, :, None]
        outs.append((cg[:, 1:] - lagged) / cnt - ug[:, :, g])
    d = jnp.stack(outs, axis=2)
    y = jnp.einsum('bsgc,gcd->bsgd', d, w_pool.astype(jnp.float32))
    return (y.reshape(B, S, POOL_WIDTH) * scale.astype(jnp.float32)).astype(u.dtype)


def moba_attention(q, k, v, rel_bias):
    B, S, H, Dh = q.shape
    nb = -(-S // BLOCK)
    sp = nb * BLOCK
    pad = ((0, 0), (0, sp - S), (0, 0), (0, 0))
    q, k, v = [jnp.pad(a, pad).transpose(0, 2, 1, 3) for a in (q, k, v)]
    kb = k.reshape(B, H, nb, BLOCK, Dh)
    vb = v.reshape(B, H, nb, BLOCK, Dh)
    kbar = jnp.mean(kb.astype(jnp.float32), axis=3)
    pos = jnp.arange(sp)
    qblk = pos // BLOCK
    ksel = min(TOP_K_BLOCKS, nb)
    bscore = jnp.einsum('bhsd,bhnd->bhsn', q.astype(jnp.float32), kbar)
    past = jnp.arange(nb)[None, :] < qblk[:, None]
    bscore = jnp.where(past[None, None], bscore, NEG)
    _, idx = lax.top_k(bscore, ksel)
    valid = jnp.arange(ksel)[None, :] < qblk[:, None]
    tab = rel_bias.T.astype(jnp.float32)
    scale = HEAD_DIM ** -0.5
    bi = jnp.arange(B)[:, None, None, None]
    hi = jnp.arange(H)[None, :, None, None]
    hi5 = jnp.arange(H)[None, :, None, None, None]

    def chunk(ci):
        s0 = ci * QCHUNK
        qc = lax.dynamic_slice_in_dim(q, s0, QCHUNK, axis=2)
        ic = lax.dynamic_slice_in_dim(idx, s0, QCHUNK, axis=2)
        vc = lax.dynamic_slice_in_dim(valid, s0, QCHUNK, axis=0)
        qpos = s0 + jnp.arange(QCHUNK)
        kg = kb[bi, hi, ic]
        vg = vb[bi, hi, ic]
        kpos_sel = ic[..., None] * BLOCK + jnp.arange(BLOCK)
        bias_sel = tab[hi5, t5_bucket(qpos[None, None, :, None, None] - kpos_sel)]
        l_sel = jnp.einsum('bhqd,bhqjkd->bhqjk', qc, kg).astype(jnp.float32) * scale + bias_sel
        l_sel = jnp.where(vc[None, None, :, :, None], l_sel, NEG)
        ob0 = (s0 // BLOCK) * BLOCK
        ko = lax.dynamic_slice_in_dim(k, ob0, BLOCK, axis=2)
        vo = lax.dynamic_slice_in_dim(v, ob0, BLOCK, axis=2)
        rel_own = qpos[:, None] - (ob0 + jnp.arange(BLOCK))[None, :]
        bias_own = tab[:, t5_bucket(rel_own)]
        l_own = jnp.einsum('bhqd,bhkd->bhqk', qc, ko).astype(jnp.float32) * scale + bias_own[None]
        l_own = jnp.where((rel_own >= 0)[None, None], l_own, NEG)
        logits = jnp.concatenate([l_sel.reshape(B, H, QCHUNK, ksel * BLOCK), l_own], axis=-1)
        pr = jax.nn.softmax(logits, axis=-1).astype(v.dtype)
        p_sel = pr[..., :ksel * BLOCK].reshape(B, H, QCHUNK, ksel, BLOCK)
        p_own = pr[..., ksel * BLOCK:]
        return (jnp.einsum('bhqjk,bhqjkd->bhqd', p_sel, vg)
                + jnp.einsum('bhqk,bhkd->bhqd', p_own, vo))

    outs = lax.map(chunk, jnp.arange(sp // QCHUNK))
    out = outs.transpose(1, 2, 0, 3, 4).reshape(B, H, sp, Dh)[:, :, :S]
    return out.transpose(0, 2, 1, 3)


def setup_inputs(seed: int = 0) -> dict:
    key = jax.random.key(seed)
    ks = jax.random.split(key, 16)
    f32 = jnp.float32
    nrm = lambda k, shape, fan: jax.random.normal(k, shape, f32) * (fan ** -0.5)
    gain = lambda k: 1.0 + 0.02 * jax.random.normal(k, (DEPTH, D_MODEL), f32)
    return {
        "x": jax.random.normal(ks[0], (BATCH, SEQ, D_MODEL), f32),
        "p": jax.random.normal(ks[1], (DEPTH, BATCH, SEQ, PLE_DIM), f32),
        "w_in": nrm(ks[2], (DEPTH, D_MODEL, IN_COLS), D_MODEL),
        "w_pool": nrm(ks[3], (DEPTH, POOL_GROUPS, POOL_CH, POOL_CH), POOL_CH),
        "pool_scale": 1.0 + 0.02 * jax.random.normal(ks[4], (DEPTH, POOL_WIDTH), f32),
        "w_out": nrm(ks[5], (DEPTH, MIX_WIDTH, D_MODEL), MIX_WIDTH),
        "rel_bias": 0.5 * jax.random.normal(ks[6], (NUM_BUCKETS, ATTN_HEADS), f32),
        "g_mix_pre": gain(ks[7]),
        "g_mix_post": gain(ks[8]),
        "g_mlp_pre": gain(ks[9]),
        "g_mlp_post": gain(ks[10]),
        "w_up": nrm(ks[11], (DEPTH, D_MODEL, D_FF), D_MODEL),
        "w_down": nrm(ks[12], (DEPTH, D_FF, D_MODEL), D_FF),
        "w_ple_proj": nrm(ks[13], (DEPTH, PLE_DIM, D_MODEL), PLE_DIM),
        "w_ple_gate": nrm(ks[14], (DEPTH, D_MODEL, D_MODEL), D_MODEL),
    }


def reference(x, p, w_in, w_pool, pool_scale, w_out, rel_bias, g_mix_pre, g_mix_post,
              g_mlp_pre, g_mlp_post, w_up, w_down, w_ple_proj, w_ple_gate):
    B, S, _ = x.shape
    h = x
    for i in range(DEPTH):
        a = rmsnorm(h, g_mix_pre[i])
        z = a @ w_in[i]
        z_pool = z[..., :POOL_WIDTH]
        q = z[..., POOL_WIDTH:POOL_WIDTH + ATTN_WIDTH].reshape(B, S, ATTN_HEADS, HEAD_DIM)
        k = z[..., POOL_WIDTH + ATTN_WIDTH:POOL_WIDTH + 2 * ATTN_WIDTH].reshape(B, S, ATTN_HEADS, HEAD_DIM)
        v = z[..., POOL_WIDTH + 2 * ATTN_WIDTH:].reshape(B, S, ATTN_HEADS, HEAD_DIM)
        y_pool = pool_mixer(z_pool, w_pool[i], pool_scale[i])
        y_attn = moba_attention(q, k, v, rel_bias).reshape(B, S, ATTN_WIDTH)
        mix = jnp.concatenate([y_pool, y_attn.astype(y_pool.dtype)], axis=-1) @ w_out[i]
        h = h + rmsnorm(mix, g_mix_post[i])
        m = rmsnorm(h, g_mlp_pre[i])
        f = jnp.square(jax.nn.relu(m @ w_up[i])) @ w_down[i]
        h = h + rmsnorm(f, g_mlp_post[i])
        h = h + jax.nn.sigmoid(h @ w_ple_gate[i]) * (p[i] @ w_ple_proj[i])
    return h
```

```python
import functools
import math

import numpy as np
import jax
import jax.numpy as jnp
from jax import lax
from jax.experimental import pallas as pl
from jax.experimental.pallas import tpu as pltpu

D_MODEL = 1024
POOL_WIDTH = 512
POOL_WINDOWS = (2, 4, 8, 16)
POOL_CH = 128
POOL_HALO = 16
ATTN_WIDTH = 512
HEAD_DIM = 64
HEADS = 8
BLOCK = 256
TOP_K = 3
NUM_BUCKETS = 32
MAX_DISTANCE = 1024
D_FF = 4096
PLE_DIM = 256
EPS = 1e-6
NEG = -1e30
LOG2E = math.log2(math.e)
KAUG = 128
NEAR = 5

TM_IN = 512
TM_POST = 512
VMEM_LIMIT = 56 * 1024 * 1024


def _bucket_thresholds():
    n = np.arange(0, NEAR * BLOCK + BLOCK, dtype=np.int64)
    max_exact = NUM_BUCKETS // 2
    nf = np.maximum(n, 1).astype(np.float64)
    large = max_exact + (np.log(nf / max_exact) / math.log(MAX_DISTANCE / max_exact)
                         * (NUM_BUCKETS - max_exact)).astype(np.int64)
    large = np.minimum(large, NUM_BUCKETS - 1)
    bucket = np.where(n < max_exact, n, large)
    assert np.all(np.diff(bucket) >= 0)
    return [int(np.argmax(bucket >= b)) for b in range(NUM_BUCKETS)], bucket


_THRESH, _BUCKET = _bucket_thresholds()
assert _BUCKET[(NEAR - 1) * BLOCK + 1] == NUM_BUCKETS - 1


def _rms(x, g):
    ms = jnp.mean(x * x, axis=-1, keepdims=True)
    return x * lax.rsqrt(ms + EPS) * g


def _inproj_kernel(x_ref, g_ref, wp_ref, wq_ref, wk_ref, wv_ref, wpool_ref, pscale_ref,
                   ypool_ref, qT_ref, kaug_ref, kbar_ref, vT_ref, halo_scr):
    t = pl.program_id(1)
    tm = x_ref.shape[1]
    nblk = tm // BLOCK
    a = _rms(x_ref[0], g_ref[...]).astype(jnp.bfloat16)

    u = jnp.dot(a, wp_ref[...], preferred_element_type=jnp.float32)

    @pl.when(t == 0)
    def _():
        halo_scr[0:POOL_HALO, :] = jnp.zeros((POOL_HALO, POOL_WIDTH), jnp.float32)

    halo_scr[POOL_HALO:POOL_HALO + tm, :] = u
    tpos = t * tm + lax.broadcasted_iota(jnp.int32, (tm, 1), 0)
    for g, w in enumerate(POOL_WINDOWS):
        cs = slice(g * POOL_CH, (g + 1) * POOL_CH)
        ug = u[:, cs]
        acc = ug
        for k in range(1, w):
            acc = acc + halo_scr[POOL_HALO - k:POOL_HALO - k + tm, cs]
        cnt = jnp.minimum(tpos + 1, w).astype(jnp.float32)
        d = acc / cnt - ug
        y = jnp.dot(d.astype(jnp.bfloat16), wpool_ref[g], preferred_element_type=jnp.float32)
        ypool_ref[0, :, cs] = (y * pscale_ref[:, cs]).astype(ypool_ref.dtype)
    halo_scr[0:POOL_HALO, :] = u[tm - POOL_HALO:tm, :]

    zq = jnp.dot(a, wq_ref[...], preferred_element_type=jnp.float32)
    for bl in range(nblk):
        qT_ref[0, bl] = zq[bl * BLOCK:(bl + 1) * BLOCK, :].T

    zv = jnp.dot(a, wv_ref[...], preferred_element_type=jnp.float32)
    for bl in range(nblk):
        vT_ref[0, bl] = zv[bl * BLOCK:(bl + 1) * BLOCK, :].T.astype(vT_ref.dtype)

    zk = jnp.dot(a, wk_ref[...], preferred_element_type=jnp.float32)
    lane = lax.broadcasted_iota(jnp.int32, (tm, KAUG), 1)
    gblk = t * nblk + lax.broadcasted_iota(jnp.int32, (tm, KAUG), 0) // BLOCK
    onehot = jnp.where((lane == HEAD_DIM + gblk) | (lane == HEAD_DIM + 32 + gblk), 1.0, 0.0)
    for h in range(HEADS):
        kaug_ref[0, h] = (zk[:, h * KAUG:(h + 1) * KAUG] + onehot).astype(kaug_ref.dtype)
    for bl in range(nblk):
        kbar_ref[0, bl] = jnp.mean(zk[bl * BLOCK:(bl + 1) * BLOCK, :], axis=0, keepdims=True)


def _bias_tile(tab_ref, h, d):
    r = lax.broadcasted_iota(jnp.int32, (BLOCK, BLOCK), 0)
    c = lax.broadcasted_iota(jnp.int32, (BLOCK, BLOCK), 1)
    n = d * BLOCK + c - r
    lo, hi = (0 if d == 0 else (d - 1) * BLOCK + 1), d * BLOCK + BLOCK - 1
    b0 = int(_BUCKET[lo])
    tile = jnp.full((BLOCK, BLOCK), tab_ref[b0, h] * LOG2E, jnp.float32)
    for b in range(b0 + 1, NUM_BUCKETS):
        if _THRESH[b] > hi:
            break
        tile = jnp.where(n >= _THRESH[b], tab_ref[b, h] * LOG2E, tile)
    if d == 0:
        tile = jnp.where(n >= 0, tile, NEG)
    return tile


def _fold8(x):
    return x.reshape(BLOCK // 8, 8, BLOCK)


def _attn_kernel(tab_ref, qT_ref, kaug_ref, vT_ref, kbar_ref, oT_ref,
                 s_scr, bias_scr, m_scr):
    h = pl.program_id(1)
    i = pl.program_id(2)
    nb = s_scr.shape[0]

    @pl.when(i == 0)
    def _():
        for d in range(NEAR):
            bias_scr[d] = _bias_tile(tab_ref, h, d)

    qT = qT_ref[0, 0]
    kb = kbar_ref[0, :, 0, :][:, :HEAD_DIM]
    sc = jnp.dot(kb, qT, preferred_element_type=jnp.float32,
                 precision=lax.Precision.HIGHEST)
    jidx = lax.broadcasted_iota(jnp.int32, (nb, BLOCK), 0)
    past = jidx < i
    work = jnp.where(past, sc, NEG)
    chosen = jnp.zeros((nb, BLOCK), jnp.bool_)
    for _ in range(TOP_K):
        mx = jnp.max(work, axis=0, keepdims=True)
        first = jnp.min(jnp.where(work == mx, jidx, nb), axis=0, keepdims=True)
        pick = jidx == first
        chosen = chosen | pick
        work = jnp.where(pick, -jnp.inf, work)
    sel = chosen & past

    far = jidx <= i - NEAR
    cfar = jnp.full((nb, BLOCK), tab_ref[NUM_BUCKETS - 1, h] * LOG2E, jnp.float32)
    cfar_hi = cfar.astype(jnp.bfloat16).astype(jnp.float32)
    cfar_lo = cfar - cfar_hi
    code_hi = jnp.where(past, jnp.where(sel, jnp.where(far, cfar_hi, 0.0), NEG), 0.0)
    code_lo = jnp.where(sel & far, cfar_lo, 0.0)
    q_aug = jnp.concatenate(
        [(qT * (HEAD_DIM ** -0.5 * LOG2E)).astype(jnp.bfloat16),
         code_hi.astype(jnp.bfloat16), code_lo.astype(jnp.bfloat16)], axis=0)

    def scores(j):
        kblk = kaug_ref[0, 0, pl.ds(pl.multiple_of(j * BLOCK, BLOCK), BLOCK), :]
        return jnp.dot(kblk, q_aug, preferred_element_type=jnp.float32)

    def far_body(j, mpart):
        s = scores(j)
        s_scr[j] = s
        return jnp.maximum(mpart, jnp.max(_fold8(s), axis=0))

    mpart = lax.fori_loop(0, jnp.maximum(i - (NEAR - 1), 0), far_body,
                          jnp.full((8, BLOCK), NEG, jnp.float32))
    m_scr[...] = mpart
    for d in range(NEAR):
        @pl.when(i >= d)
        def _():
            j = i - d
            s = scores(j) + bias_scr[d]
            s_scr[j] = s
            m_scr[...] = jnp.maximum(m_scr[...], jnp.max(_fold8(s), axis=0))
    m = jnp.max(m_scr[...], axis=0, keepdims=True)

    def pv_body(j, carry):
        lpart, acc = carry
        p = jnp.exp2(s_scr[j] - m)
        lpart = lpart + jnp.sum(_fold8(p), axis=0)
        acc = acc + jnp.dot(vT_ref[0, j], p.astype(jnp.bfloat16),
                            preferred_element_type=jnp.float32)
        return lpart, acc

    lpart, acc = lax.fori_loop(
        0, i + 1, pv_body,
        (jnp.zeros((8, BLOCK), jnp.float32), jnp.zeros((HEAD_DIM, BLOCK), jnp.float32)))
    l = jnp.sum(lpart, axis=0, keepdims=True)
    oT_ref[0, 0] = (acc / l).astype(oT_ref.dtype)


def _post_kernel(x_ref, p_ref, ypool_ref, oT_ref, wop_ref, woa_ref, gpost_ref, gpre_ref,
                 gmpost_ref, wup_ref, wdown_ref, wgate_ref, wproj_ref, out_ref):
    nblk = oT_ref.shape[1]
    yat = jnp.concatenate([oT_ref[0, bl].T for bl in range(nblk)], axis=0)
    mix = (jnp.dot(ypool_ref[0], wop_ref[...], preferred_element_type=jnp.float32)
           + jnp.dot(yat.astype(jnp.bfloat16), woa_ref[...], preferred_element_type=jnp.float32))
    h1 = x_ref[0] + _rms(mix, gpost_ref[...])
    m = _rms(h1, gpre_ref[...]).astype(jnp.bfloat16)
    up = jnp.dot(m, wup_ref[...], preferred_element_type=jnp.float32)
    act = jnp.square(jnp.maximum(up, 0.0)).astype(jnp.bfloat16)
    f = jnp.dot(act, wdown_ref[...], preferred_element_type=jnp.float32)
    h2 = h1 + _rms(f, gmpost_ref[...])
    gate = jax.nn.sigmoid(jnp.dot(h2.astype(jnp.bfloat16), wgate_ref[...],
                                  preferred_element_type=jnp.float32))
    pe = jnp.dot(p_ref[0].astype(jnp.bfloat16), wproj_ref[...], preferred_element_type=jnp.float32)
    out_ref[0] = h2 + gate * pe


def _resident(shape):
    return pl.BlockSpec(shape, lambda *_: (0,) * len(shape), pipeline_mode=pl.Buffered(1))


def _mixers(x, w_in, w_pool, pool_scale, rel_bias, g_mix_pre):
    B, S, _ = x.shape
    nb = S // BLOCK
    bf16 = jnp.bfloat16
    f32 = jnp.float32

    wp = w_in[:, :POOL_WIDTH].astype(bf16)
    wq = w_in[:, POOL_WIDTH:POOL_WIDTH + ATTN_WIDTH].astype(bf16)
    wk = w_in[:, POOL_WIDTH + ATTN_WIDTH:POOL_WIDTH + 2 * ATTN_WIDTH]
    wk = jnp.pad(wk.reshape(D_MODEL, HEADS, HEAD_DIM), ((0, 0), (0, 0), (0, KAUG - HEAD_DIM)))
    wk = wk.reshape(D_MODEL, HEADS * KAUG).astype(bf16)
    wv = w_in[:, POOL_WIDTH + 2 * ATTN_WIDTH:].astype(bf16)

    tm = TM_IN
    nblk = tm // BLOCK
    ypool, qT, kaug, kbar, vT = pl.pallas_call(
        _inproj_kernel,
        grid=(B, S // tm),
        in_specs=[
            pl.BlockSpec((1, tm, D_MODEL), lambda b, t: (b, t, 0)),
            _resident((1, D_MODEL)),
            _resident((D_MODEL, POOL_WIDTH)),
            _resident((D_MODEL, ATTN_WIDTH)),
            _resident((D_MODEL, HEADS * KAUG)),
            _resident((D_MODEL, ATTN_WIDTH)),
            _resident((len(POOL_WINDOWS), POOL_CH, POOL_CH)),
            _resident((1, POOL_WIDTH)),
        ],
        out_specs=[
            pl.BlockSpec((1, tm, POOL_WIDTH), lambda b, t: (b, t, 0)),
            pl.BlockSpec((1, nblk, ATTN_WIDTH, BLOCK), lambda b, t: (b, t, 0, 0)),
            pl.BlockSpec((1, HEADS, tm, KAUG), lambda b, t: (b, 0, t, 0)),
            pl.BlockSpec((1, nblk, 1, HEADS * KAUG), lambda b, t: (b, t, 0, 0)),
            pl.BlockSpec((1, nblk, ATTN_WIDTH, BLOCK), lambda b, t: (b, t, 0, 0)),
        ],
        out_shape=[
            jax.ShapeDtypeStruct((B, S, POOL_WIDTH), bf16),
            jax.ShapeDtypeStruct((B, nb, ATTN_WIDTH, BLOCK), f32),
            jax.ShapeDtypeStruct((B, HEADS, S, KAUG), bf16),
            jax.ShapeDtypeStruct((B, nb, 1, HEADS * KAUG), f32),
            jax.ShapeDtypeStruct((B, nb, ATTN_WIDTH, BLOCK), bf16),
        ],
        scratch_shapes=[pltpu.VMEM((POOL_HALO + tm, POOL_WIDTH), f32)],
        compiler_params=pltpu.CompilerParams(
            dimension_semantics=("arbitrary", "arbitrary"), vmem_limit_bytes=VMEM_LIMIT),
        name="inproj_pool",
    )(x, g_mix_pre, wp, wq, wk, wv, w_pool.astype(bf16), pool_scale)

    oT = pl.pallas_call(
        _attn_kernel,
        grid=(B, HEADS, nb),
        in_specs=[
            pl.BlockSpec(memory_space=pltpu.SMEM),
            pl.BlockSpec((1, 1, HEAD_DIM, BLOCK), lambda b, h, i: (b, i, h, 0)),
            pl.BlockSpec((1, 1, S, KAUG), lambda b, h, i: (b, h, 0, 0)),
            pl.BlockSpec((1, nb, HEAD_DIM, BLOCK), lambda b, h, i: (b, 0, h, 0)),
            pl.BlockSpec((1, nb, 1, KAUG), lambda b, h, i: (b, 0, 0, h)),
        ],
        out_specs=pl.BlockSpec((1, 1, HEAD_DIM, BLOCK), lambda b, h, i: (b, i, h, 0)),
        out_shape=jax.ShapeDtypeStruct((B, nb, ATTN_WIDTH, BLOCK), f32),
        scratch_shapes=[
            pltpu.VMEM((nb, BLOCK, BLOCK), f32),
            pltpu.VMEM((NEAR, BLOCK, BLOCK), f32),
            pltpu.VMEM((8, BLOCK), f32),
        ],
        compiler_params=pltpu.CompilerParams(
            dimension_semantics=("arbitrary", "arbitrary", "arbitrary"),
            vmem_limit_bytes=VMEM_LIMIT),
        name="moba_attention",
    )(rel_bias, qT, kaug, vT, kbar)
    return ypool, oT


def _layer(x, p, w_in, w_pool, pool_scale, w_out, rel_bias, g_mix_pre, g_mix_post,
           g_mlp_pre, g_mlp_post, w_up, w_down, w_ple_proj, w_ple_gate):
    B, S, _ = x.shape
    bf16 = jnp.bfloat16
    f32 = jnp.float32
    ypool, oT = _mixers(x, w_in, w_pool, pool_scale, rel_bias, g_mix_pre)

    tm = TM_POST
    nblk = tm // BLOCK
    out = pl.pallas_call(
        _post_kernel,
        grid=(B, S // tm),
        in_specs=[
            pl.BlockSpec((1, tm, D_MODEL), lambda b, t: (b, t, 0)),
            pl.BlockSpec((1, tm, PLE_DIM), lambda b, t: (b, t, 0)),
            pl.BlockSpec((1, tm, POOL_WIDTH), lambda b, t: (b, t, 0)),
            pl.BlockSpec((1, nblk, ATTN_WIDTH, BLOCK), lambda b, t: (b, t, 0, 0)),
            _resident((POOL_WIDTH, D_MODEL)),
            _resident((ATTN_WIDTH, D_MODEL)),
            _resident((1, D_MODEL)),
            _resident((1, D_MODEL)),
            _resident((1, D_MODEL)),
            _resident((D_MODEL, D_FF)),
            _resident((D_FF, D_MODEL)),
            _resident((D_MODEL, D_MODEL)),
            _resident((PLE_DIM, D_MODEL)),
        ],
        out_specs=pl.BlockSpec((1, tm, D_MODEL), lambda b, t: (b, t, 0)),
        out_shape=jax.ShapeDtypeStruct((B, S, D_MODEL), f32),
        compiler_params=pltpu.CompilerParams(
            dimension_semantics=("arbitrary", "arbitrary"), vmem_limit_bytes=VMEM_LIMIT),
        name="post_mlp",
    )(x, p, ypool, oT, w_out[:POOL_WIDTH].astype(bf16), w_out[POOL_WIDTH:].astype(bf16),
      g_mix_post, g_mlp_pre, g_mlp_post, w_up.astype(bf16), w_down.astype(bf16),
      w_ple_gate.astype(bf16), w_ple_proj.astype(bf16))
    return out


def kernel(x, p, w_in, w_pool, pool_scale, w_out, rel_bias, g_mix_pre, g_mix_post,
           g_mlp_pre, g_mlp_post, w_up, w_down, w_ple_proj, w_ple_gate):
    depth = w_in.shape[0]
    h = x
    for i in range(depth):
        h = _layer(h, p[i], w_in[i], w_pool[i], pool_scale[i:i + 1], w_out[i], rel_bias,
                   g_mix_pre[i:i + 1], g_mix_post[i:i + 1], g_mlp_pre[i:i + 1],
                   g_mlp_post[i:i + 1], w_up[i], w_down[i], w_ple_proj[i], w_ple_gate[i])
    return h
```

```python
import math

import numpy as np
import jax
import jax.numpy as jnp
from jax import lax
from jax.experimental import pallas as pl
from jax.experimental.pallas import tpu as pltpu

D_MODEL = 1024
POOL_WIDTH = 512
POOL_WINDOWS = (2, 4, 8, 16)
POOL_CH = 128
POOL_HALO = 16
ATTN_WIDTH = 512
HEAD_DIM = 64
HEADS = 8
BLOCK = 256
TOP_K = 3
NUM_BUCKETS = 32
MAX_DISTANCE = 1024
D_FF = 4096
PLE_DIM = 256
EPS = 1e-6
NEG = -1e30
LOG2E = math.log2(math.e)
KAUG = 128
NEAR = 5
CHUNK = 4
CHUNK_KEYS = CHUNK * BLOCK

TM_IN = 512
TM_POST = 512
VMEM_LIMIT = 56 * 1024 * 1024


def _bucket_thresholds():
    n = np.arange(0, NEAR * BLOCK + BLOCK, dtype=np.int64)
    max_exact = NUM_BUCKETS // 2
    nf = np.maximum(n, 1).astype(np.float64)
    large = max_exact + (np.log(nf / max_exact) / math.log(MAX_DISTANCE / max_exact)
                         * (NUM_BUCKETS - max_exact)).astype(np.int64)
    large = np.minimum(large, NUM_BUCKETS - 1)
    bucket = np.where(n < max_exact, n, large)
    assert np.all(np.diff(bucket) >= 0)
    return [int(np.argmax(bucket >= b)) for b in range(NUM_BUCKETS)], bucket


_THRESH, _BUCKET = _bucket_thresholds()
assert _BUCKET[(NEAR - 1) * BLOCK + 1] == NUM_BUCKETS - 1
assert NEAR <= CHUNK + 1


def _rms(x, g):
    ms = jnp.mean(x * x, axis=-1, keepdims=True)
    return x * lax.rsqrt(ms + EPS) * g


def _inproj_kernel(x_ref, g_ref, wp_ref, wq_ref, wk_ref, wv_ref, wpool_ref, pscale_ref,
                   ypool_ref, qT_ref, kaug_ref, kbar_ref, vT_ref, halo_scr):
    t = pl.program_id(1)
    tm = x_ref.shape[1]
    nblk = tm // BLOCK
    a = _rms(x_ref[0], g_ref[...]).astype(jnp.bfloat16)

    u = jnp.dot(a, wp_ref[...], preferred_element_type=jnp.float32)

    @pl.when(t == 0)
    def _():
        halo_scr[0:POOL_HALO, :] = jnp.zeros((POOL_HALO, POOL_WIDTH), jnp.float32)

    halo_scr[POOL_HALO:POOL_HALO + tm, :] = u
    tpos = t * tm + lax.broadcasted_iota(jnp.int32, (tm, 1), 0)
    for g, w in enumerate(POOL_WINDOWS):
        cs = slice(g * POOL_CH, (g + 1) * POOL_CH)
        ug = u[:, cs]
        acc = ug
        for k in range(1, w):
            acc = acc + halo_scr[POOL_HALO - k:POOL_HALO - k + tm, cs]
        cnt = jnp.minimum(tpos + 1, w).astype(jnp.float32)
        d = acc / cnt - ug
        y = jnp.dot(d.astype(jnp.bfloat16), wpool_ref[g], preferred_element_type=jnp.float32)
        ypool_ref[0, :, cs] = (y * pscale_ref[:, cs]).astype(ypool_ref.dtype)
    halo_scr[0:POOL_HALO, :] = u[tm - POOL_HALO:tm, :]

    zq = jnp.dot(a, wq_ref[...], preferred_element_type=jnp.float32)
    for bl in range(nblk):
        qT_ref[0, bl] = zq[bl * BLOCK:(bl + 1) * BLOCK, :].T

    zv = jnp.dot(a, wv_ref[...], preferred_element_type=jnp.float32)
    for bl in range(nblk):
        vT_ref[0, bl] = zv[bl * BLOCK:(bl + 1) * BLOCK, :].T.astype(vT_ref.dtype)

    zk = jnp.dot(a, wk_ref[...], preferred_element_type=jnp.float32)
    lane = lax.broadcasted_iota(jnp.int32, (tm, KAUG), 1)
    gblk = t * nblk + lax.broadcasted_iota(jnp.int32, (tm, KAUG), 0) // BLOCK
    onehot = jnp.where((lane == HEAD_DIM + gblk) | (lane == HEAD_DIM + 32 + gblk), 1.0, 0.0)
    for h in range(HEADS):
        kaug_ref[0, h] = (zk[:, h * KAUG:(h + 1) * KAUG] + onehot).astype(kaug_ref.dtype)
    for bl in range(nblk):
        kbar_ref[0, bl] = jnp.mean(zk[bl * BLOCK:(bl + 1) * BLOCK, :], axis=0, keepdims=True)


def _bias_tile(tab_ref, h, d):
    r = lax.broadcasted_iota(jnp.int32, (BLOCK, BLOCK), 0)
    c = lax.broadcasted_iota(jnp.int32, (BLOCK, BLOCK), 1)
    n = d * BLOCK + c - r
    lo, hi = (0 if d == 0 else (d - 1) * BLOCK + 1), d * BLOCK + BLOCK - 1
    b0 = int(_BUCKET[lo])
    tile = jnp.full((BLOCK, BLOCK), tab_ref[b0, h] * LOG2E, jnp.float32)
    for b in range(b0 + 1, NUM_BUCKETS):
        if _THRESH[b] > hi:
            break
        tile = jnp.where(n >= _THRESH[b], tab_ref[b, h] * LOG2E, tile)
    if d == 0:
        tile = jnp.where(n >= 0, tile, NEG)
    return tile


def _fold8(x):
    return x.reshape(x.shape[0] // 8, 8, x.shape[1])


def _attn_kernel(tab_ref, qT_ref, kaug_ref, vT_ref, kbar_ref, oT_ref, s_buf, bias_scr):
    h = pl.program_id(1)
    i = pl.program_id(2)
    nb = kbar_ref.shape[1]
    ci = i // CHUNK

    @pl.when(i == 0)
    def _():
        for t in range(bias_scr.shape[0]):
            d = t - (CHUNK - 1)
            if 0 <= d < NEAR:
                bias_scr[t] = _bias_tile(tab_ref, h, d)
            else:
                bias_scr[t] = jnp.zeros((BLOCK, BLOCK), jnp.float32)

    qT = qT_ref[0, 0]
    kb = kbar_ref[0, :, 0, :][:, :HEAD_DIM]
    sc = jnp.dot(kb, qT, preferred_element_type=jnp.float32,
                 precision=lax.Precision.HIGHEST)
    jidx = lax.broadcasted_iota(jnp.int32, (nb, BLOCK), 0)
    past = jidx < i
    work = jnp.where(past, sc, NEG)
    chosen = jnp.zeros((nb, BLOCK), jnp.bool_)
    for _ in range(TOP_K):
        mx = jnp.max(work, axis=0, keepdims=True)
        first = jnp.min(jnp.where(work == mx, jidx, nb), axis=0, keepdims=True)
        pick = jidx == first
        chosen = chosen | pick
        work = jnp.where(pick, -jnp.inf, work)
    sel = chosen & past

    far = jidx <= i - NEAR
    cfar = jnp.full((nb, BLOCK), tab_ref[NUM_BUCKETS - 1, h] * LOG2E, jnp.float32)
    cfar_hi = cfar.astype(jnp.bfloat16).astype(jnp.float32)
    cfar_lo = cfar - cfar_hi
    code_hi = jnp.where(past, jnp.where(sel, jnp.where(far, cfar_hi, 0.0), NEG),
                        jnp.where(jidx == i, 0.0, NEG))
    code_lo = jnp.where(sel & far, cfar_lo, 0.0)
    q_aug = jnp.concatenate(
        [(qT * (HEAD_DIM ** -0.5 * LOG2E)).astype(jnp.bfloat16),
         code_hi.astype(jnp.bfloat16), code_lo.astype(jnp.bfloat16)], axis=0)

    def qk(c, slot, near):
        kblk = kaug_ref[0, 0, pl.ds(pl.multiple_of(c * CHUNK_KEYS, CHUNK_KEYS), CHUNK_KEYS), :]
        s = jnp.dot(kblk, q_aug, preferred_element_type=jnp.float32)
        cm = None
        for w in range(CHUNK):
            sw = s[w * BLOCK:(w + 1) * BLOCK]
            if near:
                sw = sw + bias_scr[i - (c * CHUNK + w) + (CHUNK - 1)]
            s_buf[slot, w * BLOCK:(w + 1) * BLOCK, :] = sw
            fm = jnp.max(_fold8(sw), axis=0)
            cm = fm if cm is None else jnp.maximum(cm, fm)
        return cm

    def pv(slot, c, cm, m, l, acc):
        m_new = jnp.maximum(m, jnp.max(cm, axis=0, keepdims=True))
        alpha = jnp.exp2(m - m_new)
        p = jnp.exp2(s_buf[slot] - m_new)
        l = alpha * l + jnp.sum(_fold8(p), axis=0)
        vblk = jnp.concatenate([vT_ref[0, c * CHUNK + w] for w in range(CHUNK)], axis=1)
        acc = alpha * acc + jnp.dot(vblk, p.astype(jnp.bfloat16),
                                    preferred_element_type=jnp.float32)
        return m_new, l, acc

    def finish(l, acc):
        oT_ref[0, 0] = (acc / jnp.sum(l, axis=0, keepdims=True)).astype(oT_ref.dtype)

    m0 = jnp.full((1, BLOCK), NEG, jnp.float32)
    l0 = jnp.zeros((8, BLOCK), jnp.float32)
    acc0 = jnp.zeros((HEAD_DIM, BLOCK), jnp.float32)

    @pl.when(ci == 0)
    def _():
        cm = qk(0, 0, True)
        _, l, acc = pv(0, 0, cm, m0, l0, acc0)
        finish(l, acc)

    @pl.when(ci >= 1)
    def _():
        cm_own = qk(ci, 0, True)
        cm_prev = qk(ci - 1, 1, True)
        m, l, acc = pv(0, ci, cm_own, m0, l0, acc0)

        def body(c, carry):
            m, l, acc, cm = carry
            m, l, acc = pv((c + 1) % 2, jnp.where(c == 0, ci - 1, c - 1), cm, m, l, acc)
            cm = qk(c, c % 2, False)
            return m, l, acc, cm

        nfar = ci - 1
        m, l, acc, cm = lax.fori_loop(0, nfar, body, (m, l, acc, cm_prev))
        _, l, acc = pv((nfar + 1) % 2, jnp.maximum(ci - 2, 0), cm, m, l, acc)
        finish(l, acc)


def _post_kernel(x_ref, p_ref, ypool_ref, oT_ref, wop_ref, woa_ref, gpost_ref, gpre_ref,
                 gmpost_ref, wup_ref, wdown_ref, wgate_ref, wproj_ref, out_ref):
    nblk = oT_ref.shape[1]
    yat = jnp.concatenate([oT_ref[0, bl].T for bl in range(nblk)], axis=0)
    mix = (jnp.dot(ypool_ref[0], wop_ref[...], preferred_element_type=jnp.float32)
           + jnp.dot(yat.astype(jnp.bfloat16), woa_ref[...], preferred_element_type=jnp.float32))
    h1 = x_ref[0] + _rms(mix, gpost_ref[...])
    m = _rms(h1, gpre_ref[...]).astype(jnp.bfloat16)
    up = jnp.dot(m, wup_ref[...], preferred_element_type=jnp.float32)
    act = jnp.square(jnp.maximum(up, 0.0)).astype(jnp.bfloat16)
    f = jnp.dot(act, wdown_ref[...], preferred_element_type=jnp.float32)
    h2 = h1 + _rms(f, gmpost_ref[...])
    gate = jax.nn.sigmoid(jnp.dot(h2.astype(jnp.bfloat16), wgate_ref[...],
                                  preferred_element_type=jnp.float32))
    pe = jnp.dot(p_ref[0].astype(jnp.bfloat16), wproj_ref[...], preferred_element_type=jnp.float32)
    out_ref[0] = h2 + gate * pe


def _resident(shape):
    return pl.BlockSpec(shape, lambda *_: (0,) * len(shape), pipeline_mode=pl.Buffered(1))


def _mixers(x, w_in, w_pool, pool_scale, rel_bias, g_mix_pre):
    B, S, _ = x.shape
    nb = S // BLOCK
    bf16 = jnp.bfloat16
    f32 = jnp.float32

    wp = w_in[:, :POOL_WIDTH].astype(bf16)
    wq = w_in[:, POOL_WIDTH:POOL_WIDTH + ATTN_WIDTH].astype(bf16)
    wk = w_in[:, POOL_WIDTH + ATTN_WIDTH:POOL_WIDTH + 2 * ATTN_WIDTH]
    wk = jnp.pad(wk.reshape(D_MODEL, HEADS, HEAD_DIM), ((0, 0), (0, 0), (0, KAUG - HEAD_DIM)))
    wk = wk.reshape(D_MODEL, HEADS * KAUG).astype(bf16)
    wv = w_in[:, POOL_WIDTH + 2 * ATTN_WIDTH:].astype(bf16)

    tm = TM_IN
    nblk = tm // BLOCK
    ypool, qT, kaug, kbar, vT = pl.pallas_call(
        _inproj_kernel,
        grid=(B, S // tm),
        in_specs=[
            pl.BlockSpec((1, tm, D_MODEL), lambda b, t: (b, t, 0)),
            _resident((1, D_MODEL)),
            _resident((D_MODEL, POOL_WIDTH)),
            _resident((D_MODEL, ATTN_WIDTH)),
            _resident((D_MODEL, HEADS * KAUG)),
            _resident((D_MODEL, ATTN_WIDTH)),
            _resident((len(POOL_WINDOWS), POOL_CH, POOL_CH)),
            _resident((1, POOL_WIDTH)),
        ],
        out_specs=[
            pl.BlockSpec((1, tm, POOL_WIDTH), lambda b, t: (b, t, 0)),
            pl.BlockSpec((1, nblk, ATTN_WIDTH, BLOCK), lambda b, t: (b, t, 0, 0)),
            pl.BlockSpec((1, HEADS, tm, KAUG), lambda b, t: (b, 0, t, 0)),
            pl.BlockSpec((1, nblk, 1, HEADS * KAUG), lambda b, t: (b, t, 0, 0)),
            pl.BlockSpec((1, nblk, ATTN_WIDTH, BLOCK), lambda b, t: (b, t, 0, 0)),
        ],
        out_shape=[
            jax.ShapeDtypeStruct((B, S, POOL_WIDTH), bf16),
            jax.ShapeDtypeStruct((B, nb, ATTN_WIDTH, BLOCK), f32),
            jax.ShapeDtypeStruct((B, HEADS, S, KAUG), bf16),
            jax.ShapeDtypeStruct((B, nb, 1, HEADS * KAUG), f32),
            jax.ShapeDtypeStruct((B, nb, ATTN_WIDTH, BLOCK), bf16),
        ],
        scratch_shapes=[pltpu.VMEM((POOL_HALO + tm, POOL_WIDTH), f32)],
        compiler_params=pltpu.CompilerParams(
            dimension_semantics=("arbitrary", "arbitrary"), vmem_limit_bytes=VMEM_LIMIT),
        name="inproj_pool",
    )(x, g_mix_pre, wp, wq, wk, wv, w_pool.astype(bf16), pool_scale)

    oT = pl.pallas_call(
        _attn_kernel,
        grid=(B, HEADS, nb),
        in_specs=[
            pl.BlockSpec(memory_space=pltpu.SMEM),
            pl.BlockSpec((1, 1, HEAD_DIM, BLOCK), lambda b, h, i: (b, i, h, 0)),
            pl.BlockSpec((1, 1, S, KAUG), lambda b, h, i: (b, h, 0, 0)),
            pl.BlockSpec((1, nb, HEAD_DIM, BLOCK), lambda b, h, i: (b, 0, h, 0)),
            pl.BlockSpec((1, nb, 1, KAUG), lambda b, h, i: (b, 0, 0, h)),
        ],
        out_specs=pl.BlockSpec((1, 1, HEAD_DIM, BLOCK), lambda b, h, i: (b, i, h, 0)),
        out_shape=jax.ShapeDtypeStruct((B, nb, ATTN_WIDTH, BLOCK), f32),
        scratch_shapes=[
            pltpu.VMEM((2, CHUNK_KEYS, BLOCK), f32),
            pltpu.VMEM((3 * CHUNK - 1, BLOCK, BLOCK), f32),
        ],
        compiler_params=pltpu.CompilerParams(
            dimension_semantics=("arbitrary", "arbitrary", "arbitrary"),
            vmem_limit_bytes=VMEM_LIMIT),
        name="moba_attention",
    )(rel_bias, qT, kaug, vT, kbar)
    return ypool, oT


def _layer(x, p, w_in, w_pool, pool_scale, w_out, rel_bias, g_mix_pre, g_mix_post,
           g_mlp_pre, g_mlp_post, w_up, w_down, w_ple_proj, w_ple_gate):
    B, S, _ = x.shape
    bf16 = jnp.bfloat16
    f32 = jnp.float32
    ypool, oT = _mixers(x, w_in, w_pool, pool_scale, rel_bias, g_mix_pre)

    tm = TM_POST
    nblk = tm // BLOCK
    out = pl.pallas_call(
        _post_kernel,
        grid=(B, S // tm),
        in_specs=[
            pl.BlockSpec((1, tm, D_MODEL), lambda b, t: (b, t, 0)),
            pl.BlockSpec((1, tm, PLE_DIM), lambda b, t: (b, t, 0)),
            pl.BlockSpec((1, tm, POOL_WIDTH), lambda b, t: (b, t, 0)),
            pl.BlockSpec((1, nblk, ATTN_WIDTH, BLOCK), lambda b, t: (b, t, 0, 0)),
            _resident((POOL_WIDTH, D_MODEL)),
            _resident((ATTN_WIDTH, D_MODEL)),
            _resident((1, D_MODEL)),
            _resident((1, D_MODEL)),
            _resident((1, D_MODEL)),
            _resident((D_MODEL, D_FF)),
            _resident((D_FF, D_MODEL)),
            _resident((D_MODEL, D_MODEL)),
            _resident((PLE_DIM, D_MODEL)),
        ],
        out_specs=pl.BlockSpec((1, tm, D_MODEL), lambda b, t: (b, t, 0)),
        out_shape=jax.ShapeDtypeStruct((B, S, D_MODEL), f32),
        compiler_params=pltpu.CompilerParams(
            dimension_semantics=("arbitrary", "arbitrary"), vmem_limit_bytes=VMEM_LIMIT),
        name="post_mlp",
    )(x, p, ypool, oT, w_out[:POOL_WIDTH].astype(bf16), w_out[POOL_WIDTH:].astype(bf16),
      g_mix_post, g_mlp_pre, g_mlp_post, w_up.astype(bf16), w_down.astype(bf16),
      w_ple_gate.astype(bf16), w_ple_proj.astype(bf16))
    return out


def kernel(x, p, w_in, w_pool, pool_scale, w_out, rel_bias, g_mix_pre, g_mix_post,
           g_mlp_pre, g_mlp_post, w_up, w_down, w_ple_proj, w_ple_gate):
    depth = w_in.shape[0]
    h = x
    for i in range(depth):
        h = _layer(h, p[i], w_in[i], w_pool[i], pool_scale[i:i + 1], w_out[i], rel_bias,
                   g_mix_pre[i:i + 1], g_mix_post[i:i + 1], g_mlp_pre[i:i + 1],
                   g_mlp_post[i:i + 1], w_up[i], w_down[i], w_ple_proj[i], w_ple_gate[i])
    return h
```

```python
import math

import numpy as np
import jax
import jax.numpy as jnp
from jax import lax
from jax.experimental import pallas as pl
from jax.experimental.pallas import tpu as pltpu

D_MODEL = 1024
POOL_WIDTH = 512
POOL_WINDOWS = (2, 4, 8, 16)
POOL_CH = 128
POOL_HALO = 16
ATTN_WIDTH = 512
HEAD_DIM = 64
HEADS = 8
BLOCK = 256
TOP_K = 3
NUM_BUCKETS = 32
MAX_DISTANCE = 1024
D_FF = 4096
PLE_DIM = 256
EPS = 1e-6
NEG = -1e30
LOG2E = math.log2(math.e)
KAUG = 128
NEAR = 5
CHUNK = 4
CHUNK_KEYS = CHUNK * BLOCK
ACC_ROWS = HEAD_DIM + 16

TM_IN = 512
TM_POST = 512
VMEM_LIMIT = 56 * 1024 * 1024


def _bucket_thresholds():
    n = np.arange(0, NEAR * BLOCK + BLOCK, dtype=np.int64)
    max_exact = NUM_BUCKETS // 2
    nf = np.maximum(n, 1).astype(np.float64)
    large = max_exact + (np.log(nf / max_exact) / math.log(MAX_DISTANCE / max_exact)
                         * (NUM_BUCKETS - max_exact)).astype(np.int64)
    large = np.minimum(large, NUM_BUCKETS - 1)
    bucket = np.where(n < max_exact, n, large)
    assert np.all(np.diff(bucket) >= 0)
    return [int(np.argmax(bucket >= b)) for b in range(NUM_BUCKETS)], bucket


_THRESH, _BUCKET = _bucket_thresholds()
assert _BUCKET[(NEAR - 1) * BLOCK + 1] == NUM_BUCKETS - 1
assert NEAR <= CHUNK + 1


def _rms(x, g):
    ms = jnp.mean(x * x, axis=-1, keepdims=True)
    return x * lax.rsqrt(ms + EPS) * g


def _inproj_kernel(x_ref, g_ref, wp_ref, wq_ref, wk_ref, wv_ref, wpool_ref, pscale_ref,
                   ypool_ref, qT_ref, kaug_ref, kbar_ref, vT_ref, halo_scr):
    t = pl.program_id(1)
    tm = x_ref.shape[1]
    nblk = tm // BLOCK
    a = _rms(x_ref[0], g_ref[...]).astype(jnp.bfloat16)

    u = jnp.dot(a, wp_ref[...], preferred_element_type=jnp.float32)

    @pl.when(t == 0)
    def _():
        halo_scr[0:POOL_HALO, :] = jnp.zeros((POOL_HALO, POOL_WIDTH), jnp.float32)

    halo_scr[POOL_HALO:POOL_HALO + tm, :] = u
    tpos = t * tm + lax.broadcasted_iota(jnp.int32, (tm, 1), 0)
    for g, w in enumerate(POOL_WINDOWS):
        cs = slice(g * POOL_CH, (g + 1) * POOL_CH)
        ug = u[:, cs]
        acc = ug
        for k in range(1, w):
            acc = acc + halo_scr[POOL_HALO - k:POOL_HALO - k + tm, cs]
        cnt = jnp.minimum(tpos + 1, w).astype(jnp.float32)
        d = acc / cnt - ug
        y = jnp.dot(d.astype(jnp.bfloat16), wpool_ref[g], preferred_element_type=jnp.float32)
        ypool_ref[0, :, cs] = (y * pscale_ref[:, cs]).astype(ypool_ref.dtype)
    halo_scr[0:POOL_HALO, :] = u[tm - POOL_HALO:tm, :]

    zq = jnp.dot(a, wq_ref[...], preferred_element_type=jnp.float32)
    for bl in range(nblk):
        qT_ref[0, bl] = zq[bl * BLOCK:(bl + 1) * BLOCK, :].T

    zv = jnp.dot(a, wv_ref[...], preferred_element_type=jnp.float32)
    for bl in range(nblk):
        vT_ref[0, bl] = zv[bl * BLOCK:(bl + 1) * BLOCK, :].T.astype(vT_ref.dtype)

    zk = jnp.dot(a, wk_ref[...], preferred_element_type=jnp.float32)
    lane = lax.broadcasted_iota(jnp.int32, (tm, KAUG), 1)
    gblk = t * nblk + lax.broadcasted_iota(jnp.int32, (tm, KAUG), 0) // BLOCK
    onehot = jnp.where((lane == HEAD_DIM + gblk) | (lane == HEAD_DIM + 32 + gblk), 1.0, 0.0)
    for h in range(HEADS):
        kaug_ref[0, h] = (zk[:, h * KAUG:(h + 1) * KAUG] + onehot).astype(kaug_ref.dtype)
    for bl in range(nblk):
        kbar_ref[0, bl] = jnp.mean(zk[bl * BLOCK:(bl + 1) * BLOCK, :], axis=0, keepdims=True)


def _bias_tile(tab_ref, h, d):
    r = lax.broadcasted_iota(jnp.int32, (BLOCK, BLOCK), 0)
    c = lax.broadcasted_iota(jnp.int32, (BLOCK, BLOCK), 1)
    n = d * BLOCK + c - r
    lo, hi = (0 if d == 0 else (d - 1) * BLOCK + 1), d * BLOCK + BLOCK - 1
    b0 = int(_BUCKET[lo])
    tile = jnp.full((BLOCK, BLOCK), tab_ref[b0, h] * LOG2E, jnp.float32)
    for b in range(b0 + 1, NUM_BUCKETS):
        if _THRESH[b] > hi:
            break
        tile = jnp.where(n >= _THRESH[b], tab_ref[b, h] * LOG2E, tile)
    if d == 0:
        tile = jnp.where(n >= 0, tile, NEG)
    return tile


def _fold8(x):
    return x.reshape(x.shape[0] // 8, 8, x.shape[1])


def _attn_items(nb):
    near, far = [], []
    for i in range(nb):
        ci = i // CHUNK
        near.append((i, ci))
        if ci >= 1:
            near.append((i, ci - 1))
        far.extend((i, c) for c in range(ci - 1))
    return (np.asarray(near, np.int32).T.copy(), np.asarray(far, np.int32).T.copy())


def _attn_kernel(tab_ref, near_ref, far_ref, qT_ref, kaug_ref, vT_ref, kbar_ref, oT_ref,
                 qaug_scr, s_buf, p_buf, m_st, acc_st, bias_scr):
    h = pl.program_id(0)
    b = pl.program_id(1)
    nb = kbar_ref.shape[1]

    @pl.when(b == 0)
    def _():
        for t in range(bias_scr.shape[0]):
            d = t - (CHUNK - 1)
            if 0 <= d < NEAR:
                bias_scr[t] = _bias_tile(tab_ref, h, d)
            else:
                bias_scr[t] = jnp.zeros((BLOCK, BLOCK), jnp.float32)

    m_st[...] = jnp.full(m_st.shape, NEG, jnp.float32)
    acc_st[...] = jnp.zeros(acc_st.shape, jnp.float32)

    kb = kbar_ref[0, :, 0, :][:, :HEAD_DIM]
    jidx = lax.broadcasted_iota(jnp.int32, (nb, BLOCK), 0)
    cfar = jnp.full((nb, BLOCK), tab_ref[NUM_BUCKETS - 1, h] * LOG2E, jnp.float32)
    cfar_hi = cfar.astype(jnp.bfloat16).astype(jnp.float32)
    cfar_lo = cfar - cfar_hi

    def select(i, carry):
        qT = qT_ref[0, i]
        sc = jnp.dot(kb, qT, preferred_element_type=jnp.float32,
                     precision=lax.Precision.HIGHEST)
        past = jidx < i
        work = jnp.where(past, sc, NEG)
        chosen = jnp.zeros((nb, BLOCK), jnp.bool_)
        for _ in range(TOP_K):
            mx = jnp.max(work, axis=0, keepdims=True)
            first = jnp.min(jnp.where(work == mx, jidx, nb), axis=0, keepdims=True)
            pick = jidx == first
            chosen = chosen | pick
            work = jnp.where(pick, -jnp.inf, work)
        sel = chosen & past
        far = jidx <= i - NEAR
        code_hi = jnp.where(past, jnp.where(sel, jnp.where(far, cfar_hi, 0.0), NEG),
                            jnp.where(jidx == i, 0.0, NEG))
        code_lo = jnp.where(sel & far, cfar_lo, 0.0)
        qaug_scr[i] = jnp.concatenate(
            [(qT * (HEAD_DIM ** -0.5 * LOG2E)).astype(jnp.bfloat16),
             code_hi.astype(jnp.bfloat16), code_lo.astype(jnp.bfloat16)], axis=0)
        return carry

    lax.fori_loop(0, nb, select, 0)

    def stage_qk(item, slot, near):
        i, c = item
        q_aug = qaug_scr[i]
        cm = None
        for w in range(CHUNK):
            row0 = pl.multiple_of((c * CHUNK + w) * BLOCK, BLOCK)
            sw = jnp.dot(kaug_ref[0, 0, pl.ds(row0, BLOCK), :], q_aug,
                         preferred_element_type=jnp.float32)
            if near:
                sw = sw + bias_scr[i - (c * CHUNK + w) + (CHUNK - 1)]
            s_buf[slot, w] = sw
            fm = jnp.max(_fold8(sw), axis=0)
            cm = fm if cm is None else jnp.maximum(cm, fm)
        return cm

    def stage_exp(item, slot, cm):
        i, _ = item
        m_old = m_st[i]
        m_new = jnp.maximum(m_old, jnp.max(cm, axis=0, keepdims=True))
        alpha = jnp.exp2(m_old - m_new)
        m_st[i] = m_new
        m_row = m_new[0:1]
        for w in range(CHUNK):
            p = jnp.exp2(s_buf[slot, w] - m_row)
            p_buf[slot, w * BLOCK:(w + 1) * BLOCK, :] = p.astype(jnp.bfloat16)
        return alpha[0:1]

    ones_rows = (lax.broadcasted_iota(jnp.int32, (ACC_ROWS - HEAD_DIM, CHUNK_KEYS), 0) == 0
                 ).astype(jnp.bfloat16)

    def stage_pv(item, slot, alpha):
        i, c = item
        vblk = jnp.concatenate([vT_ref[0, c * CHUNK + w] for w in range(CHUNK)], axis=1)
        vblk = jnp.concatenate([vblk, ones_rows], axis=0)
        pv = jnp.dot(vblk, p_buf[slot], preferred_element_type=jnp.float32)
        acc_st[i] = alpha * acc_st[i] + pv

    def run(items_ref, near):
        n = items_ref.shape[1]
        assert n % 4 == 0 and n >= 8
        item = lambda x: (items_ref[0, x], items_ref[1, x])

        cms = [stage_qk(item(x), x, near) for x in range(4)]
        al0 = stage_exp(item(0), 0, cms[0])
        al1 = stage_exp(item(1), 1, cms[1])

        def body(k, carry):
            cm2, cm3, al0, al1 = carry
            t = 4 * k
            stage_pv(item(t - 4), 0, al0)
            al2 = stage_exp(item(t - 2), 2, cm2)
            cm0 = stage_qk(item(t), 0, near)
            stage_pv(item(t - 3), 1, al1)
            al3 = stage_exp(item(t - 1), 3, cm3)
            cm1 = stage_qk(item(t + 1), 1, near)
            stage_pv(item(t - 2), 2, al2)
            al0 = stage_exp(item(t), 0, cm0)
            cm2 = stage_qk(item(t + 2), 2, near)
            stage_pv(item(t - 1), 3, al3)
            al1 = stage_exp(item(t + 1), 1, cm1)
            cm3 = stage_qk(item(t + 3), 3, near)
            return cm2, cm3, al0, al1

        cm2, cm3, al0, al1 = lax.fori_loop(1, n // 4, body, (cms[2], cms[3], al0, al1))
        stage_pv(item(n - 4), 0, al0)
        al2 = stage_exp(item(n - 2), 2, cm2)
        stage_pv(item(n - 3), 1, al1)
        al3 = stage_exp(item(n - 1), 3, cm3)
        stage_pv(item(n - 2), 2, al2)
        stage_pv(item(n - 1), 3, al3)

    run(near_ref, True)
    run(far_ref, False)

    def finish(i, carry):
        acc = acc_st[i]
        oT_ref[0, i] = (acc[:HEAD_DIM] / acc[HEAD_DIM:HEAD_DIM + 1]).astype(oT_ref.dtype)
        return carry

    lax.fori_loop(0, nb, finish, 0)


def _post_kernel(x_ref, p_ref, ypool_ref, oT_ref, wop_ref, woa_ref, gpost_ref, gpre_ref,
                 gmpost_ref, wup_ref, wdown_ref, wgate_ref, wproj_ref, out_ref):
    nblk = oT_ref.shape[1]
    yat = jnp.concatenate([oT_ref[0, bl].T for bl in range(nblk)], axis=0)
    mix = (jnp.dot(ypool_ref[0], wop_ref[...], preferred_element_type=jnp.float32)
           + jnp.dot(yat.astype(jnp.bfloat16), woa_ref[...], preferred_element_type=jnp.float32))
    h1 = x_ref[0] + _rms(mix, gpost_ref[...])
    m = _rms(h1, gpre_ref[...]).astype(jnp.bfloat16)
    up = jnp.dot(m, wup_ref[...], preferred_element_type=jnp.float32)
    act = jnp.square(jnp.maximum(up, 0.0)).astype(jnp.bfloat16)
    f = jnp.dot(act, wdown_ref[...], preferred_element_type=jnp.float32)
    h2 = h1 + _rms(f, gmpost_ref[...])
    gate = jax.nn.sigmoid(jnp.dot(h2.astype(jnp.bfloat16), wgate_ref[...],
                                  preferred_element_type=jnp.float32))
    pe = jnp.dot(p_ref[0].astype(jnp.bfloat16), wproj_ref[...], preferred_element_type=jnp.float32)
    out_ref[0] = h2 + gate * pe


def _resident(shape):
    return pl.BlockSpec(shape, lambda *_: (0,) * len(shape), pipeline_mode=pl.Buffered(1))


def _mixers(x, w_in, w_pool, pool_scale, rel_bias, g_mix_pre):
    B, S, _ = x.shape
    nb = S // BLOCK
    bf16 = jnp.bfloat16
    f32 = jnp.float32

    wp = w_in[:, :POOL_WIDTH].astype(bf16)
    wq = w_in[:, POOL_WIDTH:POOL_WIDTH + ATTN_WIDTH].astype(bf16)
    wk = w_in[:, POOL_WIDTH + ATTN_WIDTH:POOL_WIDTH + 2 * ATTN_WIDTH]
    wk = jnp.pad(wk.reshape(D_MODEL, HEADS, HEAD_DIM), ((0, 0), (0, 0), (0, KAUG - HEAD_DIM)))
    wk = wk.reshape(D_MODEL, HEADS * KAUG).astype(bf16)
    wv = w_in[:, POOL_WIDTH + 2 * ATTN_WIDTH:].astype(bf16)

    tm = TM_IN
    nblk = tm // BLOCK
    ypool, qT, kaug, kbar, vT = pl.pallas_call(
        _inproj_kernel,
        grid=(B, S // tm),
        in_specs=[
            pl.BlockSpec((1, tm, D_MODEL), lambda b, t: (b, t, 0)),
            _resident((1, D_MODEL)),
            _resident((D_MODEL, POOL_WIDTH)),
            _resident((D_MODEL, ATTN_WIDTH)),
            _resident((D_MODEL, HEADS * KAUG)),
            _resident((D_MODEL, ATTN_WIDTH)),
            _resident((len(POOL_WINDOWS), POOL_CH, POOL_CH)),
            _resident((1, POOL_WIDTH)),
        ],
        out_specs=[
            pl.BlockSpec((1, tm, POOL_WIDTH), lambda b, t: (b, t, 0)),
            pl.BlockSpec((1, nblk, ATTN_WIDTH, BLOCK), lambda b, t: (b, t, 0, 0)),
            pl.BlockSpec((1, HEADS, tm, KAUG), lambda b, t: (b, 0, t, 0)),
            pl.BlockSpec((1, nblk, 1, HEADS * KAUG), lambda b, t: (b, t, 0, 0)),
            pl.BlockSpec((1, nblk, ATTN_WIDTH, BLOCK), lambda b, t: (b, t, 0, 0)),
        ],
        out_shape=[
            jax.ShapeDtypeStruct((B, S, POOL_WIDTH), bf16),
            jax.ShapeDtypeStruct((B, nb, ATTN_WIDTH, BLOCK), f32),
            jax.ShapeDtypeStruct((B, HEADS, S, KAUG), bf16),
            jax.ShapeDtypeStruct((B, nb, 1, HEADS * KAUG), f32),
            jax.ShapeDtypeStruct((B, nb, ATTN_WIDTH, BLOCK), bf16),
        ],
        scratch_shapes=[pltpu.VMEM((POOL_HALO + tm, POOL_WIDTH), f32)],
        compiler_params=pltpu.CompilerParams(
            dimension_semantics=("arbitrary", "arbitrary"), vmem_limit_bytes=VMEM_LIMIT),
        name="inproj_pool",
    )(x, g_mix_pre, wp, wq, wk, wv, w_pool.astype(bf16), pool_scale)

    near_items, far_items = _attn_items(nb)
    smem = pl.BlockSpec(memory_space=pltpu.SMEM)
    oT = pl.pallas_call(
        _attn_kernel,
        grid=(HEADS, B),
        in_specs=[
            smem, smem, smem,
            pl.BlockSpec((1, nb, HEAD_DIM, BLOCK), lambda h, b: (b, 0, h, 0)),
            pl.BlockSpec((1, 1, S, KAUG), lambda h, b: (b, h, 0, 0)),
            pl.BlockSpec((1, nb, HEAD_DIM, BLOCK), lambda h, b: (b, 0, h, 0)),
            pl.BlockSpec((1, nb, 1, KAUG), lambda h, b: (b, 0, 0, h)),
        ],
        out_specs=pl.BlockSpec((1, nb, HEAD_DIM, BLOCK), lambda h, b: (b, 0, h, 0)),
        out_shape=jax.ShapeDtypeStruct((B, nb, ATTN_WIDTH, BLOCK), f32),
        scratch_shapes=[
            pltpu.VMEM((nb, KAUG, BLOCK), bf16),
            pltpu.VMEM((4, CHUNK, BLOCK, BLOCK), f32),
            pltpu.VMEM((4, CHUNK_KEYS, BLOCK), bf16),
            pltpu.VMEM((nb, 8, BLOCK), f32),
            pltpu.VMEM((nb, ACC_ROWS, BLOCK), f32),
            pltpu.VMEM((3 * CHUNK - 1, BLOCK, BLOCK), f32),
        ],
        compiler_params=pltpu.CompilerParams(
            dimension_semantics=("arbitrary", "arbitrary"), vmem_limit_bytes=VMEM_LIMIT),
        name="moba_attention",
    )(rel_bias, jnp.asarray(near_items), jnp.asarray(far_items), qT, kaug, vT, kbar)
    return ypool, oT


def _layer(x, p, w_in, w_pool, pool_scale, w_out, rel_bias, g_mix_pre, g_mix_post,
           g_mlp_pre, g_mlp_post, w_up, w_down, w_ple_proj, w_ple_gate):
    B, S, _ = x.shape
    bf16 = jnp.bfloat16
    f32 = jnp.float32
    ypool, oT = _mixers(x, w_in, w_pool, pool_scale, rel_bias, g_mix_pre)

    tm = TM_POST
    nblk = tm // BLOCK
    out = pl.pallas_call(
        _post_kernel,
        grid=(B, S // tm),
        in_specs=[
            pl.BlockSpec((1, tm, D_MODEL), lambda b, t: (b, t, 0)),
            pl.BlockSpec((1, tm, PLE_DIM), lambda b, t: (b, t, 0)),
            pl.BlockSpec((1, tm, POOL_WIDTH), lambda b, t: (b, t, 0)),
            pl.BlockSpec((1, nblk, ATTN_WIDTH, BLOCK), lambda b, t: (b, t, 0, 0)),
            _resident((POOL_WIDTH, D_MODEL)),
            _resident((ATTN_WIDTH, D_MODEL)),
            _resident((1, D_MODEL)),
            _resident((1, D_MODEL)),
            _resident((1, D_MODEL)),
            _resident((D_MODEL, D_FF)),
            _resident((D_FF, D_MODEL)),
            _resident((D_MODEL, D_MODEL)),
            _resident((PLE_DIM, D_MODEL)),
        ],
        out_specs=pl.BlockSpec((1, tm, D_MODEL), lambda b, t: (b, t, 0)),
        out_shape=jax.ShapeDtypeStruct((B, S, D_MODEL), f32),
        compiler_params=pltpu.CompilerParams(
            dimension_semantics=("arbitrary", "arbitrary"), vmem_limit_bytes=VMEM_LIMIT),
        name="post_mlp",
    )(x, p, ypool, oT, w_out[:POOL_WIDTH].astype(bf16), w_out[POOL_WIDTH:].astype(bf16),
      g_mix_post, g_mlp_pre, g_mlp_post, w_up.astype(bf16), w_down.astype(bf16),
      w_ple_gate.astype(bf16), w_ple_proj.astype(bf16))
    return out


def kernel(x, p, w_in, w_pool, pool_scale, w_out, rel_bias, g_mix_pre, g_mix_post,
           g_mlp_pre, g_mlp_post, w_up, w_down, w_ple_proj, w_ple_gate):
    depth = w_in.shape[0]
    h = x
    for i in range(depth):
        h = _layer(h, p[i], w_in[i], w_pool[i], pool_scale[i:i + 1], w_out[i], rel_bias,
                   g_mix_pre[i:i + 1], g_mix_post[i:i + 1], g_mlp_pre[i:i + 1],
                   g_mlp_post[i:i + 1], w_up[i], w_down[i], w_ple_proj[i], w_ple_gate[i])
    return h
```

```python
import math

import numpy as np
import jax
import jax.numpy as jnp
from jax import lax
from jax.experimental import pallas as pl
from jax.experimental.pallas import tpu as pltpu

D_MODEL = 1024
POOL_WIDTH = 512
POOL_WINDOWS = (2, 4, 8, 16)
POOL_CH = 128
POOL_HALO = 16
ATTN_WIDTH = 512
HEAD_DIM = 64
HEADS = 8
BLOCK = 256
TOP_K = 3
NUM_BUCKETS = 32
MAX_DISTANCE = 1024
D_FF = 4096
PLE_DIM = 256
EPS = 1e-6
NEG = -1e30
LOG2E = math.log2(math.e)
KAUG = 128
NEAR = 5
CHUNK = 4
CHUNK_KEYS = CHUNK * BLOCK
ACC_ROWS = HEAD_DIM + 16
SELECT_UNROLL = 4
LAG = 6

TM_IN = 512
TM_POST = 512
VMEM_LIMIT = 56 * 1024 * 1024


def _bucket_thresholds():
    n = np.arange(0, NEAR * BLOCK + BLOCK, dtype=np.int64)
    max_exact = NUM_BUCKETS // 2
    nf = np.maximum(n, 1).astype(np.float64)
    large = max_exact + (np.log(nf / max_exact) / math.log(MAX_DISTANCE / max_exact)
                         * (NUM_BUCKETS - max_exact)).astype(np.int64)
    large = np.minimum(large, NUM_BUCKETS - 1)
    bucket = np.where(n < max_exact, n, large)
    assert np.all(np.diff(bucket) >= 0)
    return [int(np.argmax(bucket >= b)) for b in range(NUM_BUCKETS)], bucket


_THRESH, _BUCKET = _bucket_thresholds()
assert _BUCKET[(NEAR - 1) * BLOCK + 1] == NUM_BUCKETS - 1
assert NEAR <= CHUNK + 1


def _rms(x, g):
    ms = jnp.mean(x * x, axis=-1, keepdims=True)
    return x * lax.rsqrt(ms + EPS) * g


def _inproj_kernel(x_ref, g_ref, wp_ref, wq_ref, wk_ref, wv_ref, wpool_ref, pscale_ref,
                   ypool_ref, qT_ref, kaug_ref, kbar_ref, vT_ref, halo_scr):
    t = pl.program_id(1)
    tm = x_ref.shape[1]
    nblk = tm // BLOCK
    a = _rms(x_ref[0], g_ref[...]).astype(jnp.bfloat16)

    u = jnp.dot(a, wp_ref[...], preferred_element_type=jnp.float32)

    @pl.when(t == 0)
    def _():
        halo_scr[0:POOL_HALO, :] = jnp.zeros((POOL_HALO, POOL_WIDTH), jnp.float32)

    halo_scr[POOL_HALO:POOL_HALO + tm, :] = u
    tpos = t * tm + lax.broadcasted_iota(jnp.int32, (tm, 1), 0)
    for g, w in enumerate(POOL_WINDOWS):
        cs = slice(g * POOL_CH, (g + 1) * POOL_CH)
        ug = u[:, cs]
        acc = ug
        for k in range(1, w):
            acc = acc + halo_scr[POOL_HALO - k:POOL_HALO - k + tm, cs]
        cnt = jnp.minimum(tpos + 1, w).astype(jnp.float32)
        d = acc / cnt - ug
        y = jnp.dot(d.astype(jnp.bfloat16), wpool_ref[g], preferred_element_type=jnp.float32)
        ypool_ref[0, :, cs] = (y * pscale_ref[:, cs]).astype(ypool_ref.dtype)
    halo_scr[0:POOL_HALO, :] = u[tm - POOL_HALO:tm, :]

    zq = jnp.dot(a, wq_ref[...], preferred_element_type=jnp.float32)
    for bl in range(nblk):
        qT_ref[0, bl] = zq[bl * BLOCK:(bl + 1) * BLOCK, :].T

    zv = jnp.dot(a, wv_ref[...], preferred_element_type=jnp.float32)
    for bl in range(nblk):
        vT_ref[0, bl] = zv[bl * BLOCK:(bl + 1) * BLOCK, :].T.astype(vT_ref.dtype)

    zk = jnp.dot(a, wk_ref[...], preferred_element_type=jnp.float32)
    lane = lax.broadcasted_iota(jnp.int32, (tm, KAUG), 1)
    gblk = t * nblk + lax.broadcasted_iota(jnp.int32, (tm, KAUG), 0) // BLOCK
    onehot = jnp.where((lane == HEAD_DIM + gblk) | (lane == HEAD_DIM + 32 + gblk), 1.0, 0.0)
    for h in range(HEADS):
        kaug_ref[0, h] = (zk[:, h * KAUG:(h + 1) * KAUG] + onehot).astype(kaug_ref.dtype)
    for bl in range(nblk):
        kbar_ref[0, bl] = jnp.mean(zk[bl * BLOCK:(bl + 1) * BLOCK, :], axis=0, keepdims=True)


def _bias_tile(tab_ref, h, d):
    r = lax.broadcasted_iota(jnp.int32, (BLOCK, BLOCK), 0)
    c = lax.broadcasted_iota(jnp.int32, (BLOCK, BLOCK), 1)
    n = d * BLOCK + c - r
    lo, hi = (0 if d == 0 else (d - 1) * BLOCK + 1), d * BLOCK + BLOCK - 1
    b0 = int(_BUCKET[lo])
    tile = jnp.full((BLOCK, BLOCK), tab_ref[b0, h] * LOG2E, jnp.float32)
    for b in range(b0 + 1, NUM_BUCKETS):
        if _THRESH[b] > hi:
            break
        tile = jnp.where(n >= _THRESH[b], tab_ref[b, h] * LOG2E, tile)
    if d == 0:
        tile = jnp.where(n >= 0, tile, NEG)
    return tile


def _fold8(x):
    return x.reshape(x.shape[0] // 8, 8, x.shape[1])


def _attn_items(nb):
    near, far = [], []
    for i in range(nb):
        ci = i // CHUNK
        near.append((i, ci))
        if ci >= 1:
            near.append((i, ci - 1))
        far.extend((i, c) for c in range(ci - 1))
    return (np.asarray(near, np.int32).T.copy(), np.asarray(far, np.int32).T.copy())


def _attn_kernel(tab_ref, near_ref, far_ref, qT_ref, kaug_ref, vT_ref, kbar_ref, oT_ref,
                 qaug_scr, s_buf, p_buf, m_st, acc_st, bias_scr):
    h = pl.program_id(0)
    b = pl.program_id(1)
    nb = kbar_ref.shape[1]

    @pl.when(b == 0)
    def _():
        for t in range(bias_scr.shape[0]):
            d = t - (CHUNK - 1)
            if 0 <= d < NEAR:
                bias_scr[t] = _bias_tile(tab_ref, h, d)
            else:
                bias_scr[t] = jnp.zeros((BLOCK, BLOCK), jnp.float32)

    m_st[...] = jnp.full(m_st.shape, NEG, jnp.float32)
    acc_st[...] = jnp.zeros(acc_st.shape, jnp.float32)

    kb = kbar_ref[0, :, 0, :][:, :HEAD_DIM]
    jidx = lax.broadcasted_iota(jnp.int32, (nb, BLOCK), 0)
    cfar = jnp.full((nb, BLOCK), tab_ref[NUM_BUCKETS - 1, h] * LOG2E, jnp.float32)
    cfar_hi = cfar.astype(jnp.bfloat16).astype(jnp.float32)
    cfar_lo = cfar - cfar_hi

    def select(i):
        qT = qT_ref[0, i]
        sc = jnp.dot(kb, qT, preferred_element_type=jnp.float32,
                     precision=lax.Precision.HIGHEST)
        past = jidx < i
        work = jnp.where(past, sc, NEG)
        chosen = jnp.zeros((nb, BLOCK), jnp.bool_)
        for _ in range(TOP_K):
            mx = jnp.max(work, axis=0, keepdims=True)
            first = jnp.min(jnp.where(work == mx, jidx, nb), axis=0, keepdims=True)
            pick = jidx == first
            chosen = chosen | pick
            work = jnp.where(pick, -jnp.inf, work)
        sel = chosen & past
        far = jidx <= i - NEAR
        code_hi = jnp.where(past, jnp.where(sel, jnp.where(far, cfar_hi, 0.0), NEG),
                            jnp.where(jidx == i, 0.0, NEG))
        code_lo = jnp.where(sel & far, cfar_lo, 0.0)
        qaug_scr[i] = jnp.concatenate(
            [(qT * (HEAD_DIM ** -0.5 * LOG2E)).astype(jnp.bfloat16),
             code_hi.astype(jnp.bfloat16), code_lo.astype(jnp.bfloat16)], axis=0)

    def select_group(g, carry):
        for u in range(SELECT_UNROLL):
            select(g * SELECT_UNROLL + u)
        return carry

    lax.fori_loop(0, nb // SELECT_UNROLL, select_group, 0)

    def stage_qk(item, slot, near):
        i, c = item
        q_aug = qaug_scr[i]
        cm = None
        for w in range(CHUNK):
            row0 = pl.multiple_of((c * CHUNK + w) * BLOCK, BLOCK)
            sw = jnp.dot(kaug_ref[0, 0, pl.ds(row0, BLOCK), :], q_aug,
                         preferred_element_type=jnp.float32)
            if near:
                sw = sw + bias_scr[i - (c * CHUNK + w) + (CHUNK - 1)]
            s_buf[slot, w] = sw
            fm = jnp.max(_fold8(sw), axis=0)
            cm = fm if cm is None else jnp.maximum(cm, fm)
        return cm

    def stage_exp(item, slot, cm):
        i, _ = item
        m_old = m_st[i]
        m_new = jnp.maximum(m_old, jnp.max(cm, axis=0, keepdims=True))
        alpha = jnp.exp2(m_old - m_new)
        m_st[i] = m_new
        m_row = m_new[0:1]
        for w in range(CHUNK):
            p = jnp.exp2(s_buf[slot, w] - m_row)
            p_buf[slot, w * BLOCK:(w + 1) * BLOCK, :] = p.astype(jnp.bfloat16)
        return alpha[0:1]

    ones_rows = (lax.broadcasted_iota(jnp.int32, (ACC_ROWS - HEAD_DIM, CHUNK_KEYS), 0) == 0
                 ).astype(jnp.bfloat16)

    def stage_pv(item, slot, alpha):
        i, c = item
        vblk = jnp.concatenate([vT_ref[0, c * CHUNK + w] for w in range(CHUNK)], axis=1)
        vblk = jnp.concatenate([vblk, ones_rows], axis=0)
        pv = jnp.dot(vblk, p_buf[slot], preferred_element_type=jnp.float32)
        acc_st[i] = alpha * acc_st[i] + pv

    def run(items_ref, near):
        n = items_ref.shape[1]
        assert n % LAG == 0 and n >= 2 * LAG
        item = lambda x: (items_ref[0, x], items_ref[1, x])

        def softmax_pv(x, slot, cm):
            stage_pv(item(x), slot, stage_exp(item(x), slot, cm))

        cms = tuple(stage_qk(item(x), x, near) for x in range(LAG))

        def body(k, cms):
            t = LAG * k
            new = []
            for j in range(LAG):
                softmax_pv(t - LAG + j, j, cms[j])
                new.append(stage_qk(item(t + j), j, near))
            return tuple(new)

        cms = lax.fori_loop(1, n // LAG, body, cms)
        for j in range(LAG):
            softmax_pv(n - LAG + j, j, cms[j])

    run(near_ref, True)
    run(far_ref, False)

    def finish(i, carry):
        acc = acc_st[i]
        oT_ref[0, i] = (acc[:HEAD_DIM] / acc[HEAD_DIM:HEAD_DIM + 1]).astype(oT_ref.dtype)
        return carry

    lax.fori_loop(0, nb, finish, 0)


def _post_kernel(x_ref, p_ref, ypool_ref, oT_ref, wop_ref, woa_ref, gpost_ref, gpre_ref,
                 gmpost_ref, wup_ref, wdown_ref, wgate_ref, wproj_ref, out_ref):
    nblk = oT_ref.shape[1]
    yat = jnp.concatenate([oT_ref[0, bl].T for bl in range(nblk)], axis=0)
    mix = (jnp.dot(ypool_ref[0], wop_ref[...], preferred_element_type=jnp.float32)
           + jnp.dot(yat.astype(jnp.bfloat16), woa_ref[...], preferred_element_type=jnp.float32))
    h1 = x_ref[0] + _rms(mix, gpost_ref[...])
    m = _rms(h1, gpre_ref[...]).astype(jnp.bfloat16)
    up = jnp.dot(m, wup_ref[...], preferred_element_type=jnp.float32)
    act = jnp.square(jnp.maximum(up, 0.0)).astype(jnp.bfloat16)
    f = jnp.dot(act, wdown_ref[...], preferred_element_type=jnp.float32)
    h2 = h1 + _rms(f, gmpost_ref[...])
    gate = jax.nn.sigmoid(jnp.dot(h2.astype(jnp.bfloat16), wgate_ref[...],
                                  preferred_element_type=jnp.float32))
    pe = jnp.dot(p_ref[0].astype(jnp.bfloat16), wproj_ref[...], preferred_element_type=jnp.float32)
    out_ref[0] = h2 + gate * pe


def _resident(shape):
    return pl.BlockSpec(shape, lambda *_: (0,) * len(shape), pipeline_mode=pl.Buffered(1))


def _mixers(x, w_in, w_pool, pool_scale, rel_bias, g_mix_pre):
    B, S, _ = x.shape
    nb = S // BLOCK
    bf16 = jnp.bfloat16
    f32 = jnp.float32

    wp = w_in[:, :POOL_WIDTH].astype(bf16)
    wq = w_in[:, POOL_WIDTH:POOL_WIDTH + ATTN_WIDTH].astype(bf16)
    wk = w_in[:, POOL_WIDTH + ATTN_WIDTH:POOL_WIDTH + 2 * ATTN_WIDTH]
    wk = jnp.pad(wk.reshape(D_MODEL, HEADS, HEAD_DIM), ((0, 0), (0, 0), (0, KAUG - HEAD_DIM)))
    wk = wk.reshape(D_MODEL, HEADS * KAUG).astype(bf16)
    wv = w_in[:, POOL_WIDTH + 2 * ATTN_WIDTH:].astype(bf16)

    tm = TM_IN
    nblk = tm // BLOCK
    ypool, qT, kaug, kbar, vT = pl.pallas_call(
        _inproj_kernel,
        grid=(B, S // tm),
        in_specs=[
            pl.BlockSpec((1, tm, D_MODEL), lambda b, t: (b, t, 0)),
            _resident((1, D_MODEL)),
            _resident((D_MODEL, POOL_WIDTH)),
            _resident((D_MODEL, ATTN_WIDTH)),
            _resident((D_MODEL, HEADS * KAUG)),
            _resident((D_MODEL, ATTN_WIDTH)),
            _resident((len(POOL_WINDOWS), POOL_CH, POOL_CH)),
            _resident((1, POOL_WIDTH)),
        ],
        out_specs=[
            pl.BlockSpec((1, tm, POOL_WIDTH), lambda b, t: (b, t, 0)),
            pl.BlockSpec((1, nblk, ATTN_WIDTH, BLOCK), lambda b, t: (b, t, 0, 0)),
            pl.BlockSpec((1, HEADS, tm, KAUG), lambda b, t: (b, 0, t, 0)),
            pl.BlockSpec((1, nblk, 1, HEADS * KAUG), lambda b, t: (b, t, 0, 0)),
            pl.BlockSpec((1, nblk, ATTN_WIDTH, BLOCK), lambda b, t: (b, t, 0, 0)),
        ],
        out_shape=[
            jax.ShapeDtypeStruct((B, S, POOL_WIDTH), bf16),
            jax.ShapeDtypeStruct((B, nb, ATTN_WIDTH, BLOCK), f32),
            jax.ShapeDtypeStruct((B, HEADS, S, KAUG), bf16),
            jax.ShapeDtypeStruct((B, nb, 1, HEADS * KAUG), f32),
            jax.ShapeDtypeStruct((B, nb, ATTN_WIDTH, BLOCK), bf16),
        ],
        scratch_shapes=[pltpu.VMEM((POOL_HALO + tm, POOL_WIDTH), f32)],
        compiler_params=pltpu.CompilerParams(
            dimension_semantics=("arbitrary", "arbitrary"), vmem_limit_bytes=VMEM_LIMIT),
        name="inproj_pool",
    )(x, g_mix_pre, wp, wq, wk, wv, w_pool.astype(bf16), pool_scale)

    near_items, far_items = _attn_items(nb)
    smem = pl.BlockSpec(memory_space=pltpu.SMEM)
    oT = pl.pallas_call(
        _attn_kernel,
        grid=(HEADS, B),
        in_specs=[
            smem, smem, smem,
            pl.BlockSpec((1, nb, HEAD_DIM, BLOCK), lambda h, b: (b, 0, h, 0)),
            pl.BlockSpec((1, 1, S, KAUG), lambda h, b: (b, h, 0, 0)),
            pl.BlockSpec((1, nb, HEAD_DIM, BLOCK), lambda h, b: (b, 0, h, 0)),
            pl.BlockSpec((1, nb, 1, KAUG), lambda h, b: (b, 0, 0, h)),
        ],
        out_specs=pl.BlockSpec((1, nb, HEAD_DIM, BLOCK), lambda h, b: (b, 0, h, 0)),
        out_shape=jax.ShapeDtypeStruct((B, nb, ATTN_WIDTH, BLOCK), f32),
        scratch_shapes=[
            pltpu.VMEM((nb, KAUG, BLOCK), bf16),
            pltpu.VMEM((LAG, CHUNK, BLOCK, BLOCK), f32),
            pltpu.VMEM((LAG, CHUNK_KEYS, BLOCK), bf16),
            pltpu.VMEM((nb, 8, BLOCK), f32),
            pltpu.VMEM((nb, ACC_ROWS, BLOCK), f32),
            pltpu.VMEM((3 * CHUNK - 1, BLOCK, BLOCK), f32),
        ],
        compiler_params=pltpu.CompilerParams(
            dimension_semantics=("arbitrary", "arbitrary"), vmem_limit_bytes=VMEM_LIMIT),
        name="moba_attention",
    )(rel_bias, jnp.asarray(near_items), jnp.asarray(far_items), qT, kaug, vT, kbar)
    return ypool, oT


def _layer(x, p, w_in, w_pool, pool_scale, w_out, rel_bias, g_mix_pre, g_mix_post,
           g_mlp_pre, g_mlp_post, w_up, w_down, w_ple_proj, w_ple_gate):
    B, S, _ = x.shape
    bf16 = jnp.bfloat16
    f32 = jnp.float32
    ypool, oT = _mixers(x, w_in, w_pool, pool_scale, rel_bias, g_mix_pre)

    tm = TM_POST
    nblk = tm // BLOCK
    out = pl.pallas_call(
        _post_kernel,
        grid=(B, S // tm),
        in_specs=[
            pl.BlockSpec((1, tm, D_MODEL), lambda b, t: (b, t, 0)),
            pl.BlockSpec((1, tm, PLE_DIM), lambda b, t: (b, t, 0)),
            pl.BlockSpec((1, tm, POOL_WIDTH), lambda b, t: (b, t, 0)),
            pl.BlockSpec((1, nblk, ATTN_WIDTH, BLOCK), lambda b, t: (b, t, 0, 0)),
            _resident((POOL_WIDTH, D_MODEL)),
            _resident((ATTN_WIDTH, D_MODEL)),
            _resident((1, D_MODEL)),
            _resident((1, D_MODEL)),
            _resident((1, D_MODEL)),
            _resident((D_MODEL, D_FF)),
            _resident((D_FF, D_MODEL)),
            _resident((D_MODEL, D_MODEL)),
            _resident((PLE_DIM, D_MODEL)),
        ],
        out_specs=pl.BlockSpec((1, tm, D_MODEL), lambda b, t: (b, t, 0)),
        out_shape=jax.ShapeDtypeStruct((B, S, D_MODEL), f32),
        compiler_params=pltpu.CompilerParams(
            dimension_semantics=("arbitrary", "arbitrary"), vmem_limit_bytes=VMEM_LIMIT),
        name="post_mlp",
    )(x, p, ypool, oT, w_out[:POOL_WIDTH].astype(bf16), w_out[POOL_WIDTH:].astype(bf16),
      g_mix_post, g_mlp_pre, g_mlp_post, w_up.astype(bf16), w_down.astype(bf16),
      w_ple_gate.astype(bf16), w_ple_proj.astype(bf16))
    return out


def kernel(x, p, w_in, w_pool, pool_scale, w_out, rel_bias, g_mix_pre, g_mix_post,
           g_mlp_pre, g_mlp_post, w_up, w_down, w_ple_proj, w_ple_gate):
    depth = w_in.shape[0]
    h = x
    for i in range(depth):
        h = _layer(h, p[i], w_in[i], w_pool[i], pool_scale[i:i + 1], w_out[i], rel_bias,
                   g_mix_pre[i:i + 1], g_mix_post[i:i + 1], g_mlp_pre[i:i + 1],
                   g_mlp_post[i:i + 1], w_up[i], w_down[i], w_ple_proj[i], w_ple_gate[i])
    return h
```

```python
import math

import numpy as np
import jax
import jax.numpy as jnp
from jax import lax
from jax.experimental import pallas as pl
from jax.experimental.pallas import tpu as pltpu

D_MODEL = 1024
POOL_WIDTH = 512
POOL_WINDOWS = (2, 4, 8, 16)
POOL_CH = 128
POOL_HALO = 16
ATTN_WIDTH = 512
HEAD_DIM = 64
HEADS = 8
BLOCK = 256
TOP_K = 3
NUM_BUCKETS = 32
MAX_DISTANCE = 1024
D_FF = 4096
PLE_DIM = 256
EPS = 1e-6
NEG = -1e30
LOG2E = math.log2(math.e)
KAUG = 128
NEAR = 5
CHUNK = 4
CHUNK_KEYS = CHUNK * BLOCK
ACC_ROWS = HEAD_DIM + 16
SELECT_UNROLL = 4
LAG = 6

TM_IN = 512
TM_POST = 512
VMEM_LIMIT = 56 * 1024 * 1024


def _bucket_thresholds():
    n = np.arange(0, NEAR * BLOCK + BLOCK, dtype=np.int64)
    max_exact = NUM_BUCKETS // 2
    nf = np.maximum(n, 1).astype(np.float64)
    large = max_exact + (np.log(nf / max_exact) / math.log(MAX_DISTANCE / max_exact)
                         * (NUM_BUCKETS - max_exact)).astype(np.int64)
    large = np.minimum(large, NUM_BUCKETS - 1)
    bucket = np.where(n < max_exact, n, large)
    assert np.all(np.diff(bucket) >= 0)
    return [int(np.argmax(bucket >= b)) for b in range(NUM_BUCKETS)], bucket


_THRESH, _BUCKET = _bucket_thresholds()
assert _BUCKET[(NEAR - 1) * BLOCK + 1] == NUM_BUCKETS - 1
assert NEAR <= CHUNK + 1


def _rms(x, g):
    ms = jnp.mean(x * x, axis=-1, keepdims=True)
    return x * lax.rsqrt(ms + EPS) * g


def _inproj_kernel(x_ref, g_ref, wp_ref, wq_ref, wk_ref, wv_ref, wpool_ref, pscale_ref,
                   ypool_ref, qT_ref, kaug_ref, kbar_ref, vT_ref, halo_scr):
    t = pl.program_id(1)
    tm = x_ref.shape[1]
    nblk = tm // BLOCK

    @pl.when(t == 0)
    def _():
        halo_scr[...] = jnp.zeros(halo_scr.shape, jnp.float32)

    a = _rms(x_ref[0], g_ref[...]).astype(jnp.bfloat16)

    zk = jnp.dot(a, wk_ref[...], preferred_element_type=jnp.float32)
    lane = lax.broadcasted_iota(jnp.int32, (tm, KAUG), 1)
    gblk = t * nblk + lax.broadcasted_iota(jnp.int32, (tm, KAUG), 0) // BLOCK
    onehot = jnp.where((lane == HEAD_DIM + gblk) | (lane == HEAD_DIM + 32 + gblk), 1.0, 0.0)
    for h in range(HEADS):
        pair = zk[:, (h // 2) * KAUG:(h // 2 + 1) * KAUG]
        if h % 2:
            pair = pltpu.roll(pair, HEAD_DIM, axis=1)
        kaug_ref[0, h] = jnp.where(lane < HEAD_DIM, pair, onehot).astype(kaug_ref.dtype)
    for bl in range(nblk):
        kbar_ref[0, bl] = jnp.mean(zk[bl * BLOCK:(bl + 1) * BLOCK, :], axis=0, keepdims=True)

    u = jnp.dot(a, wp_ref[...], preferred_element_type=jnp.float32)

    zq = jnp.dot(a, wq_ref[...], preferred_element_type=jnp.float32)
    for bl in range(nblk):
        qT_ref[0, bl] = zq[bl * BLOCK:(bl + 1) * BLOCK, :].T

    tpos = t * tm + lax.broadcasted_iota(jnp.int32, (tm, 1), 0)
    for g, w in enumerate(POOL_WINDOWS):
        cs = slice(g * POOL_CH, (g + 1) * POOL_CH)
        ug = u[:, cs]
        acc = jnp.concatenate([halo_scr[:, cs], ug], axis=0)
        span = 1
        while span < w:
            acc = acc + pltpu.roll(acc, span, axis=0)
            span *= 2
        cnt = jnp.minimum(tpos + 1, w).astype(jnp.float32)
        d = acc[POOL_HALO:] / cnt - ug
        y = jnp.dot(d.astype(jnp.bfloat16), wpool_ref[g], preferred_element_type=jnp.float32)
        ypool_ref[0, :, cs] = (y * pscale_ref[:, cs]).astype(ypool_ref.dtype)
    halo_scr[...] = u[tm - POOL_HALO:tm, :]

    zv = jnp.dot(a, wv_ref[...], preferred_element_type=jnp.float32)
    for bl in range(nblk):
        vT_ref[0, bl] = zv[bl * BLOCK:(bl + 1) * BLOCK, :].T.astype(vT_ref.dtype)


def _bias_tile(tab_ref, h, d):
    r = lax.broadcasted_iota(jnp.int32, (BLOCK, BLOCK), 0)
    c = lax.broadcasted_iota(jnp.int32, (BLOCK, BLOCK), 1)
    n = d * BLOCK + c - r
    lo, hi = (0 if d == 0 else (d - 1) * BLOCK + 1), d * BLOCK + BLOCK - 1
    b0 = int(_BUCKET[lo])
    tile = jnp.full((BLOCK, BLOCK), tab_ref[b0, h] * LOG2E, jnp.float32)
    for b in range(b0 + 1, NUM_BUCKETS):
        if _THRESH[b] > hi:
            break
        tile = jnp.where(n >= _THRESH[b], tab_ref[b, h] * LOG2E, tile)
    if d == 0:
        tile = jnp.where(n >= 0, tile, NEG)
    return tile


def _fold8(x):
    return x.reshape(x.shape[0] // 8, 8, x.shape[1])


def _attn_items(nb):
    near, far = [], []
    for i in range(nb):
        ci = i // CHUNK
        near.append((i, ci))
        if ci >= 1:
            near.append((i, ci - 1))
        far.extend((i, c) for c in range(ci - 1))
    return (np.asarray(near, np.int32).T.copy(), np.asarray(far, np.int32).T.copy())


def _attn_kernel(tab_ref, near_ref, far_ref, qT_ref, kaug_ref, vT_ref, kbar_ref, oT_ref,
                 qaug_scr, s_buf, p_buf, m_st, acc_st, bias_scr):
    h = pl.program_id(0)
    b = pl.program_id(1)
    nb = kbar_ref.shape[1]

    @pl.when(b == 0)
    def _():
        for t in range(bias_scr.shape[0]):
            d = t - (CHUNK - 1)
            if 0 <= d < NEAR:
                bias_scr[t] = _bias_tile(tab_ref, h, d)
            else:
                bias_scr[t] = jnp.zeros((BLOCK, BLOCK), jnp.float32)

    m_st[...] = jnp.full(m_st.shape, NEG, jnp.float32)
    acc_st[...] = jnp.zeros(acc_st.shape, jnp.float32)

    kb2 = kbar_ref[0, :, 0, :]
    kb = jnp.where(h % 2 == 0, kb2[:, :HEAD_DIM], kb2[:, HEAD_DIM:])
    jidx = lax.broadcasted_iota(jnp.int32, (nb, BLOCK), 0)
    cfar = jnp.full((nb, BLOCK), tab_ref[NUM_BUCKETS - 1, h] * LOG2E, jnp.float32)
    cfar_hi = cfar.astype(jnp.bfloat16).astype(jnp.float32)
    cfar_lo = cfar - cfar_hi

    def select(i):
        qT = qT_ref[0, i]
        sc = jnp.dot(kb, qT, preferred_element_type=jnp.float32,
                     precision=lax.Precision.HIGHEST)
        past = jidx < i
        work = jnp.where(past, sc, NEG)
        chosen = jnp.zeros((nb, BLOCK), jnp.bool_)
        for _ in range(TOP_K):
            mx = jnp.max(work, axis=0, keepdims=True)
            first = jnp.min(jnp.where(work == mx, jidx, nb), axis=0, keepdims=True)
            pick = jidx == first
            chosen = chosen | pick
            work = jnp.where(pick, -jnp.inf, work)
        sel = chosen & past
        far = jidx <= i - NEAR
        code_hi = jnp.where(past, jnp.where(sel, jnp.where(far, cfar_hi, 0.0), NEG),
                            jnp.where(jidx == i, 0.0, NEG))
        code_lo = jnp.where(sel & far, cfar_lo, 0.0)
        qaug_scr[i] = jnp.concatenate(
            [(qT * (HEAD_DIM ** -0.5 * LOG2E)).astype(jnp.bfloat16),
             code_hi.astype(jnp.bfloat16), code_lo.astype(jnp.bfloat16)], axis=0)

    def select_group(g, carry):
        for u in range(SELECT_UNROLL):
            select(g * SELECT_UNROLL + u)
        return carry

    lax.fori_loop(0, nb // SELECT_UNROLL, select_group, 0)

    def stage_qk(item, slot, near):
        i, c = item
        q_aug = qaug_scr[i]
        cm = None
        for w in range(CHUNK):
            row0 = pl.multiple_of((c * CHUNK + w) * BLOCK, BLOCK)
            sw = jnp.dot(kaug_ref[0, 0, pl.ds(row0, BLOCK), :], q_aug,
                         preferred_element_type=jnp.float32)
            if near:
                sw = sw + bias_scr[i - (c * CHUNK + w) + (CHUNK - 1)]
            s_buf[slot, w] = sw
            fm = jnp.max(_fold8(sw), axis=0)
            cm = fm if cm is None else jnp.maximum(cm, fm)
        return cm

    def stage_exp(item, slot, cm):
        i, _ = item
        m_old = m_st[i]
        m_new = jnp.maximum(m_old, jnp.max(cm, axis=0, keepdims=True))
        alpha = jnp.exp2(m_old - m_new)
        m_st[i] = m_new
        m_row = m_new[0:1]
        for w in range(CHUNK):
            p = jnp.exp2(s_buf[slot, w] - m_row)
            p_buf[slot, w * BLOCK:(w + 1) * BLOCK, :] = p.astype(jnp.bfloat16)
        return alpha[0:1]

    ones_rows = (lax.broadcasted_iota(jnp.int32, (ACC_ROWS - HEAD_DIM, CHUNK_KEYS), 0) == 0
                 ).astype(jnp.bfloat16)

    def stage_pv(item, slot, alpha):
        i, c = item
        vblk = jnp.concatenate([vT_ref[0, c * CHUNK + w] for w in range(CHUNK)], axis=1)
        vblk = jnp.concatenate([vblk, ones_rows], axis=0)
        pv = jnp.dot(vblk, p_buf[slot], preferred_element_type=jnp.float32)
        acc_st[i] = alpha * acc_st[i] + pv

    def run(items_ref, near):
        n = items_ref.shape[1]
        assert n % LAG == 0 and n >= 2 * LAG
        item = lambda x: (items_ref[0, x], items_ref[1, x])

        def softmax_pv(x, slot, cm):
            stage_pv(item(x), slot, stage_exp(item(x), slot, cm))

        cms = tuple(stage_qk(item(x), x, near) for x in range(LAG))

        def body(k, cms):
            t = LAG * k
            new = []
            for j in range(LAG):
                softmax_pv(t - LAG + j, j, cms[j])
                new.append(stage_qk(item(t + j), j, near))
            return tuple(new)

        cms = lax.fori_loop(1, n // LAG, body, cms)
        for j in range(LAG):
            softmax_pv(n - LAG + j, j, cms[j])

    run(near_ref, True)
    run(far_ref, False)

    def finish(i, carry):
        acc = acc_st[i]
        oT_ref[0, i] = (acc[:HEAD_DIM] / acc[HEAD_DIM:HEAD_DIM + 1]).astype(oT_ref.dtype)
        return carry

    lax.fori_loop(0, nb, finish, 0)


def _post_kernel(x_ref, p_ref, ypool_ref, oT_ref, wop_ref, woa_ref, gpost_ref, gpre_ref,
                 gmpost_ref, wup_ref, wdown_ref, wgate_ref, wproj_ref, out_ref):
    nblk = oT_ref.shape[1]
    yat = jnp.concatenate([oT_ref[0, bl].T for bl in range(nblk)], axis=0)
    mix = (jnp.dot(ypool_ref[0], wop_ref[...], preferred_element_type=jnp.float32)
           + jnp.dot(yat.astype(jnp.bfloat16), woa_ref[...], preferred_element_type=jnp.float32))
    h1 = x_ref[0] + _rms(mix, gpost_ref[...])
    m = _rms(h1, gpre_ref[...]).astype(jnp.bfloat16)
    up = jnp.dot(m, wup_ref[...], preferred_element_type=jnp.float32)
    act = jnp.square(jnp.maximum(up, 0.0)).astype(jnp.bfloat16)
    f = jnp.dot(act, wdown_ref[...], preferred_element_type=jnp.float32)
    h2 = h1 + _rms(f, gmpost_ref[...])
    gate = jax.nn.sigmoid(jnp.dot(h2.astype(jnp.bfloat16), wgate_ref[...],
                                  preferred_element_type=jnp.float32))
    pe = jnp.dot(p_ref[0].astype(jnp.bfloat16), wproj_ref[...], preferred_element_type=jnp.float32)
    out_ref[0] = h2 + gate * pe


def _resident(shape):
    return pl.BlockSpec(shape, lambda *_: (0,) * len(shape), pipeline_mode=pl.Buffered(1))


def _mixers(x, w_in, w_pool, pool_scale, rel_bias, g_mix_pre):
    B, S, _ = x.shape
    nb = S // BLOCK
    bf16 = jnp.bfloat16
    f32 = jnp.float32

    wp = w_in[:, :POOL_WIDTH].astype(bf16)
    wq = w_in[:, POOL_WIDTH:POOL_WIDTH + ATTN_WIDTH].astype(bf16)
    wk = w_in[:, POOL_WIDTH + ATTN_WIDTH:POOL_WIDTH + 2 * ATTN_WIDTH].astype(bf16)
    wv = w_in[:, POOL_WIDTH + 2 * ATTN_WIDTH:].astype(bf16)

    tm = TM_IN
    nblk = tm // BLOCK
    ypool, qT, kaug, kbar, vT = pl.pallas_call(
        _inproj_kernel,
        grid=(B, S // tm),
        in_specs=[
            pl.BlockSpec((1, tm, D_MODEL), lambda b, t: (b, t, 0)),
            _resident((1, D_MODEL)),
            _resident((D_MODEL, POOL_WIDTH)),
            _resident((D_MODEL, ATTN_WIDTH)),
            _resident((D_MODEL, ATTN_WIDTH)),
            _resident((D_MODEL, ATTN_WIDTH)),
            _resident((len(POOL_WINDOWS), POOL_CH, POOL_CH)),
            _resident((1, POOL_WIDTH)),
        ],
        out_specs=[
            pl.BlockSpec((1, tm, POOL_WIDTH), lambda b, t: (b, t, 0)),
            pl.BlockSpec((1, nblk, ATTN_WIDTH, BLOCK), lambda b, t: (b, t, 0, 0)),
            pl.BlockSpec((1, HEADS, tm, KAUG), lambda b, t: (b, 0, t, 0)),
            pl.BlockSpec((1, nblk, 1, ATTN_WIDTH), lambda b, t: (b, t, 0, 0)),
            pl.BlockSpec((1, nblk, ATTN_WIDTH, BLOCK), lambda b, t: (b, t, 0, 0)),
        ],
        out_shape=[
            jax.ShapeDtypeStruct((B, S, POOL_WIDTH), bf16),
            jax.ShapeDtypeStruct((B, nb, ATTN_WIDTH, BLOCK), f32),
            jax.ShapeDtypeStruct((B, HEADS, S, KAUG), bf16),
            jax.ShapeDtypeStruct((B, nb, 1, ATTN_WIDTH), f32),
            jax.ShapeDtypeStruct((B, nb, ATTN_WIDTH, BLOCK), bf16),
        ],
        scratch_shapes=[pltpu.VMEM((POOL_HALO, POOL_WIDTH), f32)],
        compiler_params=pltpu.CompilerParams(
            dimension_semantics=("arbitrary", "arbitrary"), vmem_limit_bytes=VMEM_LIMIT),
        name="inproj_pool",
    )(x, g_mix_pre, wp, wq, wk, wv, w_pool.astype(bf16), pool_scale)

    near_items, far_items = _attn_items(nb)
    smem = pl.BlockSpec(memory_space=pltpu.SMEM)
    oT = pl.pallas_call(
        _attn_kernel,
        grid=(HEADS, B),
        in_specs=[
            smem, smem, smem,
            pl.BlockSpec((1, nb, HEAD_DIM, BLOCK), lambda h, b: (b, 0, h, 0)),
            pl.BlockSpec((1, 1, S, KAUG), lambda h, b: (b, h, 0, 0)),
            pl.BlockSpec((1, nb, HEAD_DIM, BLOCK), lambda h, b: (b, 0, h, 0)),
            pl.BlockSpec((1, nb, 1, 2 * HEAD_DIM), lambda h, b: (b, 0, 0, h // 2)),
        ],
        out_specs=pl.BlockSpec((1, nb, HEAD_DIM, BLOCK), lambda h, b: (b, 0, h, 0)),
        out_shape=jax.ShapeDtypeStruct((B, nb, ATTN_WIDTH, BLOCK), f32),
        scratch_shapes=[
            pltpu.VMEM((nb, KAUG, BLOCK), bf16),
            pltpu.VMEM((LAG, CHUNK, BLOCK, BLOCK), f32),
            pltpu.VMEM((LAG, CHUNK_KEYS, BLOCK), bf16),
            pltpu.VMEM((nb, 8, BLOCK), f32),
            pltpu.VMEM((nb, ACC_ROWS, BLOCK), f32),
            pltpu.VMEM((3 * CHUNK - 1, BLOCK, BLOCK), f32),
        ],
        compiler_params=pltpu.CompilerParams(
            dimension_semantics=("arbitrary", "arbitrary"), vmem_limit_bytes=VMEM_LIMIT),
        name="moba_attention",
    )(rel_bias, jnp.asarray(near_items), jnp.asarray(far_items), qT, kaug, vT, kbar)
    return ypool, oT


def _layer(x, p, w_in, w_pool, pool_scale, w_out, rel_bias, g_mix_pre, g_mix_post,
           g_mlp_pre, g_mlp_post, w_up, w_down, w_ple_proj, w_ple_gate):
    B, S, _ = x.shape
    bf16 = jnp.bfloat16
    f32 = jnp.float32
    ypool, oT = _mixers(x, w_in, w_pool, pool_scale, rel_bias, g_mix_pre)

    tm = TM_POST
    nblk = tm // BLOCK
    out = pl.pallas_call(
        _post_kernel,
        grid=(B, S // tm),
        in_specs=[
            pl.BlockSpec((1, tm, D_MODEL), lambda b, t: (b, t, 0)),
            pl.BlockSpec((1, tm, PLE_DIM), lambda b, t: (b, t, 0)),
            pl.BlockSpec((1, tm, POOL_WIDTH), lambda b, t: (b, t, 0)),
            pl.BlockSpec((1, nblk, ATTN_WIDTH, BLOCK), lambda b, t: (b, t, 0, 0)),
            _resident((POOL_WIDTH, D_MODEL)),
            _resident((ATTN_WIDTH, D_MODEL)),
            _resident((1, D_MODEL)),
            _resident((1, D_MODEL)),
            _resident((1, D_MODEL)),
            _resident((D_MODEL, D_FF)),
            _resident((D_FF, D_MODEL)),
            _resident((D_MODEL, D_MODEL)),
            _resident((PLE_DIM, D_MODEL)),
        ],
        out_specs=pl.BlockSpec((1, tm, D_MODEL), lambda b, t: (b, t, 0)),
        out_shape=jax.ShapeDtypeStruct((B, S, D_MODEL), f32),
        compiler_params=pltpu.CompilerParams(
            dimension_semantics=("arbitrary", "arbitrary"), vmem_limit_bytes=VMEM_LIMIT),
        name="post_mlp",
    )(x, p, ypool, oT, w_out[:POOL_WIDTH].astype(bf16), w_out[POOL_WIDTH:].astype(bf16),
      g_mix_post, g_mlp_pre, g_mlp_post, w_up.astype(bf16), w_down.astype(bf16),
      w_ple_gate.astype(bf16), w_ple_proj.astype(bf16))
    return out


def kernel(x, p, w_in, w_pool, pool_scale, w_out, rel_bias, g_mix_pre, g_mix_post,
           g_mlp_pre, g_mlp_post, w_up, w_down, w_ple_proj, w_ple_gate):
    depth = w_in.shape[0]
    h = x
    for i in range(depth):
        h = _layer(h, p[i], w_in[i], w_pool[i], pool_scale[i:i + 1], w_out[i], rel_bias,
                   g_mix_pre[i:i + 1], g_mix_post[i:i + 1], g_mlp_pre[i:i + 1],
                   g_mlp_post[i:i + 1], w_up[i], w_down[i], w_ple_proj[i], w_ple_gate[i])
    return h
```

```python
import math

import numpy as np
import jax
import jax.numpy as jnp
from jax import lax
from jax.experimental import pallas as pl
from jax.experimental.pallas import tpu as pltpu

D_MODEL = 1024
POOL_WIDTH = 512
POOL_WINDOWS = (2, 4, 8, 16)
POOL_CH = 128
POOL_HALO = 16
ATTN_WIDTH = 512
HEAD_DIM = 64
HEADS = 8
BLOCK = 256
TOP_K = 3
NUM_BUCKETS = 32
MAX_DISTANCE = 1024
D_FF = 4096
PLE_DIM = 256
EPS = 1e-6
NEG = -1e30
LOG2E = math.log2(math.e)
KAUG = 128
NEAR = 5
CHUNK = 4
CHUNK_KEYS = CHUNK * BLOCK
ACC_ROWS = HEAD_DIM + 16
SELECT_UNROLL = 4
LAG_PREV = 7
LAG_FAR = 6
LAG_MAX = max(CHUNK, LAG_PREV, LAG_FAR)

TM_IN = 512
TM_POST = 512
VMEM_LIMIT = 56 * 1024 * 1024


def _bucket_thresholds():
    n = np.arange(0, NEAR * BLOCK + BLOCK, dtype=np.int64)
    max_exact = NUM_BUCKETS // 2
    nf = np.maximum(n, 1).astype(np.float64)
    large = max_exact + (np.log(nf / max_exact) / math.log(MAX_DISTANCE / max_exact)
                         * (NUM_BUCKETS - max_exact)).astype(np.int64)
    large = np.minimum(large, NUM_BUCKETS - 1)
    bucket = np.where(n < max_exact, n, large)
    assert np.all(np.diff(bucket) >= 0)
    return [int(np.argmax(bucket >= b)) for b in range(NUM_BUCKETS)], bucket


_THRESH, _BUCKET = _bucket_thresholds()
assert _BUCKET[(NEAR - 1) * BLOCK + 1] == NUM_BUCKETS - 1
assert NEAR <= CHUNK + 1


def _rms(x, g):
    ms = jnp.mean(x * x, axis=-1, keepdims=True)
    return x * lax.rsqrt(ms + EPS) * g


def _inproj_kernel(x_ref, g_ref, wp_ref, wq_ref, wk_ref, wv_ref, wpool_ref, pscale_ref,
                   ypool_ref, qT_ref, kaug_ref, kbar_ref, vT_ref, halo_scr):
    t = pl.program_id(1)
    tm = x_ref.shape[1]
    nblk = tm // BLOCK

    @pl.when(t == 0)
    def _():
        halo_scr[...] = jnp.zeros(halo_scr.shape, jnp.float32)

    a = _rms(x_ref[0], g_ref[...]).astype(jnp.bfloat16)

    zk = jnp.dot(a, wk_ref[...], preferred_element_type=jnp.float32)
    lane = lax.broadcasted_iota(jnp.int32, (tm, KAUG), 1)
    gblk = t * nblk + lax.broadcasted_iota(jnp.int32, (tm, KAUG), 0) // BLOCK
    onehot = jnp.where((lane == HEAD_DIM + gblk) | (lane == HEAD_DIM + 32 + gblk), 1.0, 0.0)
    for h in range(HEADS):
        pair = zk[:, (h // 2) * KAUG:(h // 2 + 1) * KAUG]
        if h % 2:
            pair = pltpu.roll(pair, HEAD_DIM, axis=1)
        kaug_ref[0, h] = jnp.where(lane < HEAD_DIM, pair, onehot).astype(kaug_ref.dtype)
    for bl in range(nblk):
        kbar_ref[0, bl] = jnp.mean(zk[bl * BLOCK:(bl + 1) * BLOCK, :], axis=0, keepdims=True)

    u = jnp.dot(a, wp_ref[...], preferred_element_type=jnp.float32)

    zq = jnp.dot(a, wq_ref[...], preferred_element_type=jnp.float32)
    for bl in range(nblk):
        qT_ref[0, bl] = zq[bl * BLOCK:(bl + 1) * BLOCK, :].T

    tpos = t * tm + lax.broadcasted_iota(jnp.int32, (tm, 1), 0)
    for g, w in enumerate(POOL_WINDOWS):
        cs = slice(g * POOL_CH, (g + 1) * POOL_CH)
        ug = u[:, cs]
        acc = jnp.concatenate([halo_scr[:, cs], ug], axis=0)
        span = 1
        while span < w:
            acc = acc + pltpu.roll(acc, span, axis=0)
            span *= 2
        cnt = jnp.minimum(tpos + 1, w).astype(jnp.float32)
        d = acc[POOL_HALO:] / cnt - ug
        y = jnp.dot(d.astype(jnp.bfloat16), wpool_ref[g], preferred_element_type=jnp.float32)
        ypool_ref[0, :, cs] = (y * pscale_ref[:, cs]).astype(ypool_ref.dtype)
    halo_scr[...] = u[tm - POOL_HALO:tm, :]

    zv = jnp.dot(a, wv_ref[...], preferred_element_type=jnp.float32)
    for bl in range(nblk):
        vT_ref[0, bl] = zv[bl * BLOCK:(bl + 1) * BLOCK, :].T.astype(vT_ref.dtype)


def _bias_tile(tab_ref, h, d):
    r = lax.broadcasted_iota(jnp.int32, (BLOCK, BLOCK), 0)
    c = lax.broadcasted_iota(jnp.int32, (BLOCK, BLOCK), 1)
    n = d * BLOCK + c - r
    lo, hi = (0 if d == 0 else (d - 1) * BLOCK + 1), d * BLOCK + BLOCK - 1
    b0 = int(_BUCKET[lo])
    tile = jnp.full((BLOCK, BLOCK), tab_ref[b0, h] * LOG2E, jnp.float32)
    for b in range(b0 + 1, NUM_BUCKETS):
        if _THRESH[b] > hi:
            break
        tile = jnp.where(n >= _THRESH[b], tab_ref[b, h] * LOG2E, tile)
    if d == 0:
        tile = jnp.where(n >= 0, tile, NEG)
    return tile


def _fold8(x):
    return x.reshape(x.shape[0] // 8, 8, x.shape[1])


def _attn_items(nb):
    own = [(i, i // CHUNK) for i in range(nb)]
    prev = [(i, i // CHUNK - 1) for i in range(CHUNK, nb)]
    far = [(i, c) for i in range(nb) for c in range(i // CHUNK - 1)]
    return tuple(np.asarray(v, np.int32).T.copy() for v in (own, prev, far))


def _attn_kernel(tab_ref, own_ref, prev_ref, far_ref, qT_ref, kaug_ref, vT_ref, kbar_ref, oT_ref,
                 qaug_scr, s_buf, p_buf, m_st, acc_st, bias_scr):
    h = pl.program_id(0)
    b = pl.program_id(1)
    nb = kbar_ref.shape[1]

    @pl.when(b == 0)
    def _():
        for d in range(bias_scr.shape[0]):
            if d < NEAR:
                bias_scr[d] = _bias_tile(tab_ref, h, d)
            else:
                bias_scr[d] = jnp.zeros((BLOCK, BLOCK), jnp.float32)

    m_st[...] = jnp.full(m_st.shape, NEG, jnp.float32)
    acc_st[...] = jnp.zeros(acc_st.shape, jnp.float32)

    kb2 = kbar_ref[0, :, 0, :]
    kb = jnp.where(h % 2 == 0, kb2[:, :HEAD_DIM], kb2[:, HEAD_DIM:])
    jidx = lax.broadcasted_iota(jnp.int32, (nb, BLOCK), 0)
    cfar = jnp.full((nb, BLOCK), tab_ref[NUM_BUCKETS - 1, h] * LOG2E, jnp.float32)
    cfar_hi = cfar.astype(jnp.bfloat16).astype(jnp.float32)
    cfar_lo = cfar - cfar_hi

    def select(i):
        qT = qT_ref[0, i]
        sc = jnp.dot(kb, qT, preferred_element_type=jnp.float32,
                     precision=lax.Precision.HIGHEST)
        past = jidx < i
        work = jnp.where(past, sc, NEG)
        chosen = jnp.zeros((nb, BLOCK), jnp.bool_)
        for _ in range(TOP_K):
            mx = jnp.max(work, axis=0, keepdims=True)
            first = jnp.min(jnp.where(work == mx, jidx, nb), axis=0, keepdims=True)
            pick = jidx == first
            chosen = chosen | pick
            work = jnp.where(pick, -jnp.inf, work)
        sel = chosen & past
        far = jidx <= i - NEAR
        code_hi = jnp.where(past, jnp.where(sel, jnp.where(far, cfar_hi, 0.0), NEG),
                            jnp.where(jidx == i, 0.0, NEG))
        code_lo = jnp.where(sel & far, cfar_lo, 0.0)
        qaug_scr[i] = jnp.concatenate(
            [(qT * (HEAD_DIM ** -0.5 * LOG2E)).astype(jnp.bfloat16),
             code_hi.astype(jnp.bfloat16), code_lo.astype(jnp.bfloat16)], axis=0)

    def select_group(g, carry):
        for u in range(SELECT_UNROLL):
            select(g * SELECT_UNROLL + u)
        return carry

    lax.fori_loop(0, nb // SELECT_UNROLL, select_group, 0)

    def stage_qk(item, slot, near, nblk):
        i, c = item
        q_aug = qaug_scr[i]
        cm = None
        for w in range(nblk):
            row0 = pl.multiple_of((c * CHUNK + w) * BLOCK, BLOCK)
            sw = jnp.dot(kaug_ref[0, 0, pl.ds(row0, BLOCK), :], q_aug,
                         preferred_element_type=jnp.float32)
            if near:
                sw = sw + bias_scr[i - (c * CHUNK + w)]
            s_buf[slot, w] = sw
            fm = jnp.max(_fold8(sw), axis=0)
            cm = fm if cm is None else jnp.maximum(cm, fm)
        return cm

    def stage_exp(item, slot, cm, nblk):
        i, _ = item
        m_old = m_st[i]
        m_new = jnp.maximum(m_old, jnp.max(cm, axis=0, keepdims=True))
        alpha = jnp.exp2(m_old - m_new)
        m_st[i] = m_new
        m_row = m_new[0:1]
        for w in range(nblk):
            p = jnp.exp2(s_buf[slot, w] - m_row)
            p_buf[slot, w * BLOCK:(w + 1) * BLOCK, :] = p.astype(jnp.bfloat16)
        return alpha[0:1]

    def ones_rows(keys):
        first = lax.broadcasted_iota(jnp.int32, (ACC_ROWS - HEAD_DIM, keys), 0) == 0
        return first.astype(jnp.bfloat16)

    def stage_pv(item, slot, alpha, nblk):
        i, c = item
        vblk = jnp.concatenate([vT_ref[0, c * CHUNK + w] for w in range(nblk)], axis=1)
        vblk = jnp.concatenate([vblk, ones_rows(nblk * BLOCK)], axis=0)
        pv = jnp.dot(vblk, p_buf[slot, :nblk * BLOCK, :],
                     preferred_element_type=jnp.float32)
        acc_st[i] = alpha * acc_st[i] + pv

    def run(items_ref, near, lag, nblk_of_slot=lambda j: CHUNK):
        n = items_ref.shape[1]
        assert n % lag == 0 and n >= 2 * lag and lag <= s_buf.shape[0]
        item = lambda x: (items_ref[0, x], items_ref[1, x])

        def softmax_pv(x, j, cm):
            stage_pv(item(x), j, stage_exp(item(x), j, cm, nblk_of_slot(j)), nblk_of_slot(j))

        cms = tuple(stage_qk(item(x), x, near, nblk_of_slot(x)) for x in range(lag))

        def body(k, cms):
            t = lag * k
            new = []
            for j in range(lag):
                softmax_pv(t - lag + j, j, cms[j])
                new.append(stage_qk(item(t + j), j, near, nblk_of_slot(j)))
            return tuple(new)

        cms = lax.fori_loop(1, n // lag, body, cms)
        for j in range(lag):
            softmax_pv(n - lag + j, j, cms[j])

    run(own_ref, True, CHUNK, lambda j: j + 1)
    run(prev_ref, True, LAG_PREV)
    run(far_ref, False, LAG_FAR)

    def finish(i, carry):
        acc = acc_st[i]
        oT_ref[0, i] = (acc[:HEAD_DIM] / acc[HEAD_DIM:HEAD_DIM + 1]).astype(oT_ref.dtype)
        return carry

    lax.fori_loop(0, nb, finish, 0)


def _post_kernel(x_ref, p_ref, ypool_ref, oT_ref, wop_ref, woa_ref, gpost_ref, gpre_ref,
                 gmpost_ref, wup_ref, wdown_ref, wgate_ref, wproj_ref, out_ref):
    nblk = oT_ref.shape[1]
    yat = jnp.concatenate([oT_ref[0, bl].T for bl in range(nblk)], axis=0)
    mix = (jnp.dot(ypool_ref[0], wop_ref[...], preferred_element_type=jnp.float32)
           + jnp.dot(yat.astype(jnp.bfloat16), woa_ref[...], preferred_element_type=jnp.float32))
    h1 = x_ref[0] + _rms(mix, gpost_ref[...])
    m = _rms(h1, gpre_ref[...]).astype(jnp.bfloat16)
    up = jnp.dot(m, wup_ref[...], preferred_element_type=jnp.float32)
    act = jnp.square(jnp.maximum(up, 0.0)).astype(jnp.bfloat16)
    f = jnp.dot(act, wdown_ref[...], preferred_element_type=jnp.float32)
    h2 = h1 + _rms(f, gmpost_ref[...])
    gate = jax.nn.sigmoid(jnp.dot(h2.astype(jnp.bfloat16), wgate_ref[...],
                                  preferred_element_type=jnp.float32))
    pe = jnp.dot(p_ref[0].astype(jnp.bfloat16), wproj_ref[...], preferred_element_type=jnp.float32)
    out_ref[0] = h2 + gate * pe


def _resident(shape):
    return pl.BlockSpec(shape, lambda *_: (0,) * len(shape), pipeline_mode=pl.Buffered(1))


def _mixers(x, w_in, w_pool, pool_scale, rel_bias, g_mix_pre):
    B, S, _ = x.shape
    nb = S // BLOCK
    bf16 = jnp.bfloat16
    f32 = jnp.float32

    wp = w_in[:, :POOL_WIDTH].astype(bf16)
    wq = w_in[:, POOL_WIDTH:POOL_WIDTH + ATTN_WIDTH].astype(bf16)
    wk = w_in[:, POOL_WIDTH + ATTN_WIDTH:POOL_WIDTH + 2 * ATTN_WIDTH].astype(bf16)
    wv = w_in[:, POOL_WIDTH + 2 * ATTN_WIDTH:].astype(bf16)

    tm = TM_IN
    nblk = tm // BLOCK
    ypool, qT, kaug, kbar, vT = pl.pallas_call(
        _inproj_kernel,
        grid=(B, S // tm),
        in_specs=[
            pl.BlockSpec((1, tm, D_MODEL), lambda b, t: (b, t, 0)),
            _resident((1, D_MODEL)),
            _resident((D_MODEL, POOL_WIDTH)),
            _resident((D_MODEL, ATTN_WIDTH)),
            _resident((D_MODEL, ATTN_WIDTH)),
            _resident((D_MODEL, ATTN_WIDTH)),
            _resident((len(POOL_WINDOWS), POOL_CH, POOL_CH)),
            _resident((1, POOL_WIDTH)),
        ],
        out_specs=[
            pl.BlockSpec((1, tm, POOL_WIDTH), lambda b, t: (b, t, 0)),
            pl.BlockSpec((1, nblk, ATTN_WIDTH, BLOCK), lambda b, t: (b, t, 0, 0)),
            pl.BlockSpec((1, HEADS, tm, KAUG), lambda b, t: (b, 0, t, 0)),
            pl.BlockSpec((1, nblk, 1, ATTN_WIDTH), lambda b, t: (b, t, 0, 0)),
            pl.BlockSpec((1, nblk, ATTN_WIDTH, BLOCK), lambda b, t: (b, t, 0, 0)),
        ],
        out_shape=[
            jax.ShapeDtypeStruct((B, S, POOL_WIDTH), bf16),
            jax.ShapeDtypeStruct((B, nb, ATTN_WIDTH, BLOCK), f32),
            jax.ShapeDtypeStruct((B, HEADS, S, KAUG), bf16),
            jax.ShapeDtypeStruct((B, nb, 1, ATTN_WIDTH), f32),
            jax.ShapeDtypeStruct((B, nb, ATTN_WIDTH, BLOCK), bf16),
        ],
        scratch_shapes=[pltpu.VMEM((POOL_HALO, POOL_WIDTH), f32)],
        compiler_params=pltpu.CompilerParams(
            dimension_semantics=("arbitrary", "arbitrary"), vmem_limit_bytes=VMEM_LIMIT),
        name="inproj_pool",
    )(x, g_mix_pre, wp, wq, wk, wv, w_pool.astype(bf16), pool_scale)

    own_items, prev_items, far_items = _attn_items(nb)
    smem = pl.BlockSpec(memory_space=pltpu.SMEM)
    oT = pl.pallas_call(
        _attn_kernel,
        grid=(HEADS, B),
        in_specs=[
            smem, smem, smem, smem,
            pl.BlockSpec((1, nb, HEAD_DIM, BLOCK), lambda h, b: (b, 0, h, 0)),
            pl.BlockSpec((1, 1, S, KAUG), lambda h, b: (b, h, 0, 0)),
            pl.BlockSpec((1, nb, HEAD_DIM, BLOCK), lambda h, b: (b, 0, h, 0)),
            pl.BlockSpec((1, nb, 1, 2 * HEAD_DIM), lambda h, b: (b, 0, 0, h // 2)),
        ],
        out_specs=pl.BlockSpec((1, nb, HEAD_DIM, BLOCK), lambda h, b: (b, 0, h, 0)),
        out_shape=jax.ShapeDtypeStruct((B, nb, ATTN_WIDTH, BLOCK), f32),
        scratch_shapes=[
            pltpu.VMEM((nb, KAUG, BLOCK), bf16),
            pltpu.VMEM((LAG_MAX, CHUNK, BLOCK, BLOCK), f32),
            pltpu.VMEM((LAG_MAX, CHUNK_KEYS, BLOCK), bf16),
            pltpu.VMEM((nb, 8, BLOCK), f32),
            pltpu.VMEM((nb, ACC_ROWS, BLOCK), f32),
            pltpu.VMEM((2 * CHUNK, BLOCK, BLOCK), f32),
        ],
        compiler_params=pltpu.CompilerParams(
            dimension_semantics=("arbitrary", "arbitrary"), vmem_limit_bytes=VMEM_LIMIT),
        name="moba_attention",
    )(rel_bias, jnp.asarray(own_items), jnp.asarray(prev_items), jnp.asarray(far_items),
      qT, kaug, vT, kbar)
    return ypool, oT


def _layer(x, p, w_in, w_pool, pool_scale, w_out, rel_bias, g_mix_pre, g_mix_post,
           g_mlp_pre, g_mlp_post, w_up, w_down, w_ple_proj, w_ple_gate):
    B, S, _ = x.shape
    bf16 = jnp.bfloat16
    f32 = jnp.float32
    ypool, oT = _mixers(x, w_in, w_pool, pool_scale, rel_bias, g_mix_pre)

    tm = TM_POST
    nblk = tm // BLOCK
    out = pl.pallas_call(
        _post_kernel,
        grid=(B, S // tm),
        in_specs=[
            pl.BlockSpec((1, tm, D_MODEL), lambda b, t: (b, t, 0)),
            pl.BlockSpec((1, tm, PLE_DIM), lambda b, t: (b, t, 0)),
            pl.BlockSpec((1, tm, POOL_WIDTH), lambda b, t: (b, t, 0)),
            pl.BlockSpec((1, nblk, ATTN_WIDTH, BLOCK), lambda b, t: (b, t, 0, 0)),
            _resident((POOL_WIDTH, D_MODEL)),
            _resident((ATTN_WIDTH, D_MODEL)),
            _resident((1, D_MODEL)),
            _resident((1, D_MODEL)),
            _resident((1, D_MODEL)),
            _resident((D_MODEL, D_FF)),
            _resident((D_FF, D_MODEL)),
            _resident((D_MODEL, D_MODEL)),
            _resident((PLE_DIM, D_MODEL)),
        ],
        out_specs=pl.BlockSpec((1, tm, D_MODEL), lambda b, t: (b, t, 0)),
        out_shape=jax.ShapeDtypeStruct((B, S, D_MODEL), f32),
        compiler_params=pltpu.CompilerParams(
            dimension_semantics=("arbitrary", "arbitrary"), vmem_limit_bytes=VMEM_LIMIT),
        name="post_mlp",
    )(x, p, ypool, oT, w_out[:POOL_WIDTH].astype(bf16), w_out[POOL_WIDTH:].astype(bf16),
      g_mix_post, g_mlp_pre, g_mlp_post, w_up.astype(bf16), w_down.astype(bf16),
      w_ple_gate.astype(bf16), w_ple_proj.astype(bf16))
    return out


def kernel(x, p, w_in, w_pool, pool_scale, w_out, rel_bias, g_mix_pre, g_mix_post,
           g_mlp_pre, g_mlp_post, w_up, w_down, w_ple_proj, w_ple_gate):
    depth = w_in.shape[0]
    h = x
    for i in range(depth):
        h = _layer(h, p[i], w_in[i], w_pool[i], pool_scale[i:i + 1], w_out[i], rel_bias,
                   g_mix_pre[i:i + 1], g_mix_post[i:i + 1], g_mlp_pre[i:i + 1],
                   g_mlp_post[i:i + 1], w_up[i], w_down[i], w_ple_proj[i], w_ple_gate[i])
    return h
```

```python
import math

import numpy as np
import jax
import jax.numpy as jnp
from jax import lax
from jax.experimental import pallas as pl
from jax.experimental.pallas import tpu as pltpu

D_MODEL = 1024
POOL_WIDTH = 512
POOL_WINDOWS = (2, 4, 8, 16)
POOL_CH = 128
POOL_HALO = 16
ATTN_WIDTH = 512
HEAD_DIM = 64
HEADS = 8
BLOCK = 256
TOP_K = 3
NUM_BUCKETS = 32
MAX_DISTANCE = 1024
D_FF = 4096
PLE_DIM = 256
EPS = 1e-6
NEG = -1e30
LOG2E = math.log2(math.e)
KAUG = 128
NEAR = 5
CHUNK = 4
CHUNK_KEYS = CHUNK * BLOCK
ACC_ROWS = HEAD_DIM + 16
SELECT_UNROLL = 4
LAG_PREV = 7
LAG_FAR = 12
LAG_MAX = max(CHUNK, LAG_PREV, LAG_FAR)

TM_IN = 512
TM_POST = 512
VMEM_LIMIT = 56 * 1024 * 1024


def _bucket_thresholds():
    n = np.arange(0, NEAR * BLOCK + BLOCK, dtype=np.int64)
    max_exact = NUM_BUCKETS // 2
    nf = np.maximum(n, 1).astype(np.float64)
    large = max_exact + (np.log(nf / max_exact) / math.log(MAX_DISTANCE / max_exact)
                         * (NUM_BUCKETS - max_exact)).astype(np.int64)
    large = np.minimum(large, NUM_BUCKETS - 1)
    bucket = np.where(n < max_exact, n, large)
    assert np.all(np.diff(bucket) >= 0)
    return [int(np.argmax(bucket >= b)) for b in range(NUM_BUCKETS)], bucket


_THRESH, _BUCKET = _bucket_thresholds()
assert _BUCKET[(NEAR - 1) * BLOCK + 1] == NUM_BUCKETS - 1
assert NEAR <= CHUNK + 1


def _rms(x, g):
    ms = jnp.mean(x * x, axis=-1, keepdims=True)
    return x * lax.rsqrt(ms + EPS) * g


def _inproj_kernel(x_ref, g_ref, wp_ref, wq_ref, wk_ref, wv_ref, wpool_ref, pscale_ref,
                   ypool_ref, qT_ref, kaug_ref, kbar_ref, vT_ref, halo_scr):
    t = pl.program_id(1)
    tm = x_ref.shape[1]
    nblk = tm // BLOCK

    @pl.when(t == 0)
    def _():
        halo_scr[...] = jnp.zeros(halo_scr.shape, jnp.float32)

    a = _rms(x_ref[0], g_ref[...]).astype(jnp.bfloat16)

    zk = jnp.dot(a, wk_ref[...], preferred_element_type=jnp.float32)
    lane = lax.broadcasted_iota(jnp.int32, (tm, KAUG), 1)
    gblk = t * nblk + lax.broadcasted_iota(jnp.int32, (tm, KAUG), 0) // BLOCK
    onehot = jnp.where((lane == HEAD_DIM + gblk) | (lane == HEAD_DIM + 32 + gblk), 1.0, 0.0)
    for h in range(HEADS):
        pair = zk[:, (h // 2) * KAUG:(h // 2 + 1) * KAUG]
        if h % 2:
            pair = pltpu.roll(pair, HEAD_DIM, axis=1)
        kaug_ref[0, h] = jnp.where(lane < HEAD_DIM, pair, onehot).astype(kaug_ref.dtype)
    for bl in range(nblk):
        kbar_ref[0, bl] = jnp.mean(zk[bl * BLOCK:(bl + 1) * BLOCK, :], axis=0, keepdims=True)

    u = jnp.dot(a, wp_ref[...], preferred_element_type=jnp.float32)

    zq = jnp.dot(a, wq_ref[...], preferred_element_type=jnp.float32)
    for bl in range(nblk):
        qT_ref[0, bl] = zq[bl * BLOCK:(bl + 1) * BLOCK, :].T

    tpos = t * tm + lax.broadcasted_iota(jnp.int32, (tm, 1), 0)
    for g, w in enumerate(POOL_WINDOWS):
        cs = slice(g * POOL_CH, (g + 1) * POOL_CH)
        ug = u[:, cs]
        acc = jnp.concatenate([halo_scr[:, cs], ug], axis=0)
        span = 1
        while span < w:
            acc = acc + pltpu.roll(acc, span, axis=0)
            span *= 2
        cnt = jnp.minimum(tpos + 1, w).astype(jnp.float32)
        d = acc[POOL_HALO:] / cnt - ug
        y = jnp.dot(d.astype(jnp.bfloat16), wpool_ref[g], preferred_element_type=jnp.float32)
        ypool_ref[0, :, cs] = (y * pscale_ref[:, cs]).astype(ypool_ref.dtype)
    halo_scr[...] = u[tm - POOL_HALO:tm, :]

    zv = jnp.dot(a, wv_ref[...], preferred_element_type=jnp.float32)
    for bl in range(nblk):
        vT_ref[0, bl] = zv[bl * BLOCK:(bl + 1) * BLOCK, :].T.astype(vT_ref.dtype)


def _bias_tile(tab_ref, h, d):
    r = lax.broadcasted_iota(jnp.int32, (BLOCK, BLOCK), 0)
    c = lax.broadcasted_iota(jnp.int32, (BLOCK, BLOCK), 1)
    n = d * BLOCK + c - r
    lo, hi = (0 if d == 0 else (d - 1) * BLOCK + 1), d * BLOCK + BLOCK - 1
    b0 = int(_BUCKET[lo])
    tile = jnp.full((BLOCK, BLOCK), tab_ref[b0, h] * LOG2E, jnp.float32)
    for b in range(b0 + 1, NUM_BUCKETS):
        if _THRESH[b] > hi:
            break
        tile = jnp.where(n >= _THRESH[b], tab_ref[b, h] * LOG2E, tile)
    if d == 0:
        tile = jnp.where(n >= 0, tile, NEG)
    return tile


def _fold8(x):
    return x.reshape(x.shape[0] // 8, 8, x.shape[1])


def _attn_items(nb):
    own = [(i, i // CHUNK) for i in range(nb)]
    prev = [(i, i // CHUNK - 1) for i in range(CHUNK, nb)]
    far = [(i, c) for i in range(nb) for c in range(i // CHUNK - 1)]
    return tuple(np.asarray(v, np.int32).T.copy() for v in (own, prev, far))


def _attn_kernel(tab_ref, own_ref, prev_ref, far_ref, qT_ref, kaug_ref, vT_ref, kbar_ref, oT_ref,
                 qaug_scr, s_buf, p_buf, m_st, acc_st, bias_scr):
    h = pl.program_id(0)
    b = pl.program_id(1)
    nb = kbar_ref.shape[1]

    @pl.when(b == 0)
    def _():
        for d in range(bias_scr.shape[0]):
            if d < NEAR:
                bias_scr[d] = _bias_tile(tab_ref, h, d)
            else:
                bias_scr[d] = jnp.zeros((BLOCK, BLOCK), jnp.float32)

    m_st[...] = jnp.full(m_st.shape, NEG, jnp.float32)
    acc_st[...] = jnp.zeros(acc_st.shape, jnp.float32)

    kb2 = kbar_ref[0, :, 0, :]
    kb = jnp.where(h % 2 == 0, kb2[:, :HEAD_DIM], kb2[:, HEAD_DIM:])
    kb_hi = kb.astype(jnp.bfloat16)
    kb_lo = (kb - kb_hi.astype(jnp.float32)).astype(jnp.bfloat16)
    jidx = lax.broadcasted_iota(jnp.int32, (nb, BLOCK), 0)
    cfar = jnp.full((nb, BLOCK), tab_ref[NUM_BUCKETS - 1, h] * LOG2E, jnp.float32)
    cfar_hi = cfar.astype(jnp.bfloat16).astype(jnp.float32)
    cfar_lo = cfar - cfar_hi

    def select(i):
        qT = qT_ref[0, i]
        q_hi = qT.astype(jnp.bfloat16)
        q_lo = (qT - q_hi.astype(jnp.float32)).astype(jnp.bfloat16)
        sc = (jnp.dot(kb_hi, q_hi, preferred_element_type=jnp.float32)
              + (jnp.dot(kb_hi, q_lo, preferred_element_type=jnp.float32)
                 + jnp.dot(kb_lo, q_hi, preferred_element_type=jnp.float32)))
        past = jidx < i
        work = jnp.where(past, sc, NEG)
        chosen = jnp.zeros((nb, BLOCK), jnp.bool_)
        for _ in range(TOP_K):
            mx = jnp.max(work, axis=0, keepdims=True)
            first = jnp.min(jnp.where(work == mx, jidx, nb), axis=0, keepdims=True)
            pick = jidx == first
            chosen = chosen | pick
            work = jnp.where(pick, -jnp.inf, work)
        sel = chosen & past
        far = jidx <= i - NEAR
        code_hi = jnp.where(past, jnp.where(sel, jnp.where(far, cfar_hi, 0.0), NEG),
                            jnp.where(jidx == i, 0.0, NEG))
        code_lo = jnp.where(sel & far, cfar_lo, 0.0)
        qaug_scr[i] = jnp.concatenate(
            [(qT * (HEAD_DIM ** -0.5 * LOG2E)).astype(jnp.bfloat16),
             code_hi.astype(jnp.bfloat16), code_lo.astype(jnp.bfloat16)], axis=0)

    def select_group(g, carry):
        for u in range(SELECT_UNROLL):
            select(g * SELECT_UNROLL + u)
        return carry

    lax.fori_loop(0, nb // SELECT_UNROLL, select_group, 0)

    def stage_qk(item, slot, near, nblk):
        i, c = item
        q_aug = qaug_scr[i]
        cm = None
        for w in range(nblk):
            row0 = pl.multiple_of((c * CHUNK + w) * BLOCK, BLOCK)
            sw = jnp.dot(kaug_ref[0, 0, pl.ds(row0, BLOCK), :], q_aug,
                         preferred_element_type=jnp.float32)
            if near:
                sw = sw + bias_scr[i - (c * CHUNK + w)]
            s_buf[slot, w] = sw
            fm = jnp.max(_fold8(sw), axis=0)
            cm = fm if cm is None else jnp.maximum(cm, fm)
        return cm

    def stage_exp(item, slot, cm, nblk):
        i, _ = item
        m_old = m_st[i]
        m_new = jnp.maximum(m_old, jnp.max(cm, axis=0, keepdims=True))
        alpha = jnp.exp2(m_old - m_new)
        m_st[i] = m_new
        m_row = m_new[0:1]
        for w in range(nblk):
            p = jnp.exp2(s_buf[slot, w] - m_row)
            p_buf[slot, w * BLOCK:(w + 1) * BLOCK, :] = p.astype(jnp.bfloat16)
        return alpha[0:1]

    def ones_rows(keys):
        first = lax.broadcasted_iota(jnp.int32, (ACC_ROWS - HEAD_DIM, keys), 0) == 0
        return first.astype(jnp.bfloat16)

    def stage_pv(item, slot, alpha, nblk):
        i, c = item
        vblk = jnp.concatenate([vT_ref[0, c * CHUNK + w] for w in range(nblk)], axis=1)
        vblk = jnp.concatenate([vblk, ones_rows(nblk * BLOCK)], axis=0)
        pv = jnp.dot(vblk, p_buf[slot, :nblk * BLOCK, :],
                     preferred_element_type=jnp.float32)
        acc_st[i] = alpha * acc_st[i] + pv

    def run(items_ref, near, lag, nblk_of_slot=lambda j: CHUNK):
        n = items_ref.shape[1]
        assert n % lag == 0 and n >= 2 * lag and lag <= s_buf.shape[0]
        item = lambda x: (items_ref[0, x], items_ref[1, x])

        def softmax_pv(x, j, cm):
            stage_pv(item(x), j, stage_exp(item(x), j, cm, nblk_of_slot(j)), nblk_of_slot(j))

        cms = tuple(stage_qk(item(x), x, near, nblk_of_slot(x)) for x in range(lag))

        def body(k, cms):
            t = lag * k
            new = []
            for j in range(lag):
                softmax_pv(t - lag + j, j, cms[j])
                new.append(stage_qk(item(t + j), j, near, nblk_of_slot(j)))
            return tuple(new)

        cms = lax.fori_loop(1, n // lag, body, cms)
        for j in range(lag):
            softmax_pv(n - lag + j, j, cms[j])

    run(own_ref, True, CHUNK, lambda j: j + 1)
    run(prev_ref, True, LAG_PREV)
    run(far_ref, False, LAG_FAR)

    def finish(i, carry):
        acc = acc_st[i]
        oT_ref[0, i] = (acc[:HEAD_DIM] / acc[HEAD_DIM:HEAD_DIM + 1]).astype(oT_ref.dtype)
        return carry

    lax.fori_loop(0, nb, finish, 0)


def _post_kernel(x_ref, p_ref, ypool_ref, oT_ref, wop_ref, woa_ref, gpost_ref, gpre_ref,
                 gmpost_ref, wup_ref, wdown_ref, wgate_ref, wproj_ref, out_ref):
    nblk = oT_ref.shape[1]
    yat = jnp.concatenate([oT_ref[0, bl].T for bl in range(nblk)], axis=0)
    mix = (jnp.dot(ypool_ref[0], wop_ref[...], preferred_element_type=jnp.float32)
           + jnp.dot(yat.astype(jnp.bfloat16), woa_ref[...], preferred_element_type=jnp.float32))
    h1 = x_ref[0] + _rms(mix, gpost_ref[...])
    m = _rms(h1, gpre_ref[...]).astype(jnp.bfloat16)
    up = jnp.dot(m, wup_ref[...], preferred_element_type=jnp.float32)
    act = jnp.square(jnp.maximum(up, 0.0)).astype(jnp.bfloat16)
    f = jnp.dot(act, wdown_ref[...], preferred_element_type=jnp.float32)
    h2 = h1 + _rms(f, gmpost_ref[...])
    gate = jax.nn.sigmoid(jnp.dot(h2.astype(jnp.bfloat16), wgate_ref[...],
                                  preferred_element_type=jnp.float32))
    pe = jnp.dot(p_ref[0].astype(jnp.bfloat16), wproj_ref[...], preferred_element_type=jnp.float32)
    out_ref[0] = h2 + gate * pe


def _resident(shape):
    return pl.BlockSpec(shape, lambda *_: (0,) * len(shape), pipeline_mode=pl.Buffered(1))


def _mixers(x, w_in, w_pool, pool_scale, rel_bias, g_mix_pre):
    B, S, _ = x.shape
    nb = S // BLOCK
    bf16 = jnp.bfloat16
    f32 = jnp.float32

    wp = w_in[:, :POOL_WIDTH].astype(bf16)
    wq = w_in[:, POOL_WIDTH:POOL_WIDTH + ATTN_WIDTH].astype(bf16)
    wk = w_in[:, POOL_WIDTH + ATTN_WIDTH:POOL_WIDTH + 2 * ATTN_WIDTH].astype(bf16)
    wv = w_in[:, POOL_WIDTH + 2 * ATTN_WIDTH:].astype(bf16)

    tm = TM_IN
    nblk = tm // BLOCK
    ypool, qT, kaug, kbar, vT = pl.pallas_call(
        _inproj_kernel,
        grid=(B, S // tm),
        in_specs=[
            pl.BlockSpec((1, tm, D_MODEL), lambda b, t: (b, t, 0)),
            _resident((1, D_MODEL)),
            _resident((D_MODEL, POOL_WIDTH)),
            _resident((D_MODEL, ATTN_WIDTH)),
            _resident((D_MODEL, ATTN_WIDTH)),
            _resident((D_MODEL, ATTN_WIDTH)),
            _resident((len(POOL_WINDOWS), POOL_CH, POOL_CH)),
            _resident((1, POOL_WIDTH)),
        ],
        out_specs=[
            pl.BlockSpec((1, tm, POOL_WIDTH), lambda b, t: (b, t, 0)),
            pl.BlockSpec((1, nblk, ATTN_WIDTH, BLOCK), lambda b, t: (b, t, 0, 0)),
            pl.BlockSpec((1, HEADS, tm, KAUG), lambda b, t: (b, 0, t, 0)),
            pl.BlockSpec((1, nblk, 1, ATTN_WIDTH), lambda b, t: (b, t, 0, 0)),
            pl.BlockSpec((1, nblk, ATTN_WIDTH, BLOCK), lambda b, t: (b, t, 0, 0)),
        ],
        out_shape=[
            jax.ShapeDtypeStruct((B, S, POOL_WIDTH), bf16),
            jax.ShapeDtypeStruct((B, nb, ATTN_WIDTH, BLOCK), f32),
            jax.ShapeDtypeStruct((B, HEADS, S, KAUG), bf16),
            jax.ShapeDtypeStruct((B, nb, 1, ATTN_WIDTH), f32),
            jax.ShapeDtypeStruct((B, nb, ATTN_WIDTH, BLOCK), bf16),
        ],
        scratch_shapes=[pltpu.VMEM((POOL_HALO, POOL_WIDTH), f32)],
        compiler_params=pltpu.CompilerParams(
            dimension_semantics=("arbitrary", "arbitrary"), vmem_limit_bytes=VMEM_LIMIT),
        name="inproj_pool",
    )(x, g_mix_pre, wp, wq, wk, wv, w_pool.astype(bf16), pool_scale)

    own_items, prev_items, far_items = _attn_items(nb)
    smem = pl.BlockSpec(memory_space=pltpu.SMEM)
    oT = pl.pallas_call(
        _attn_kernel,
        grid=(HEADS, B),
        in_specs=[
            smem, smem, smem, smem,
            pl.BlockSpec((1, nb, HEAD_DIM, BLOCK), lambda h, b: (b, 0, h, 0)),
            pl.BlockSpec((1, 1, S, KAUG), lambda h, b: (b, h, 0, 0)),
            pl.BlockSpec((1, nb, HEAD_DIM, BLOCK), lambda h, b: (b, 0, h, 0)),
            pl.BlockSpec((1, nb, 1, 2 * HEAD_DIM), lambda h, b: (b, 0, 0, h // 2)),
        ],
        out_specs=pl.BlockSpec((1, nb, HEAD_DIM, BLOCK), lambda h, b: (b, 0, h, 0)),
        out_shape=jax.ShapeDtypeStruct((B, nb, ATTN_WIDTH, BLOCK), f32),
        scratch_shapes=[
            pltpu.VMEM((nb, KAUG, BLOCK), bf16),
            pltpu.VMEM((LAG_MAX, CHUNK, BLOCK, BLOCK), f32),
            pltpu.VMEM((LAG_MAX, CHUNK_KEYS, BLOCK), bf16),
            pltpu.VMEM((nb, 8, BLOCK), f32),
            pltpu.VMEM((nb, ACC_ROWS, BLOCK), f32),
            pltpu.VMEM((2 * CHUNK, BLOCK, BLOCK), f32),
        ],
        compiler_params=pltpu.CompilerParams(
            dimension_semantics=("arbitrary", "arbitrary"), vmem_limit_bytes=VMEM_LIMIT),
        name="moba_attention",
    )(rel_bias, jnp.asarray(own_items), jnp.asarray(prev_items), jnp.asarray(far_items),
      qT, kaug, vT, kbar)
    return ypool, oT


def _layer(x, p, w_in, w_pool, pool_scale, w_out, rel_bias, g_mix_pre, g_mix_post,
           g_mlp_pre, g_mlp_post, w_up, w_down, w_ple_proj, w_ple_gate):
    B, S, _ = x.shape
    bf16 = jnp.bfloat16
    f32 = jnp.float32
    ypool, oT = _mixers(x, w_in, w_pool, pool_scale, rel_bias, g_mix_pre)

    tm = TM_POST
    nblk = tm // BLOCK
    out = pl.pallas_call(
        _post_kernel,
        grid=(B, S // tm),
        in_specs=[
            pl.BlockSpec((1, tm, D_MODEL), lambda b, t: (b, t, 0)),
            pl.BlockSpec((1, tm, PLE_DIM), lambda b, t: (b, t, 0)),
            pl.BlockSpec((1, tm, POOL_WIDTH), lambda b, t: (b, t, 0)),
            pl.BlockSpec((1, nblk, ATTN_WIDTH, BLOCK), lambda b, t: (b, t, 0, 0)),
            _resident((POOL_WIDTH, D_MODEL)),
            _resident((ATTN_WIDTH, D_MODEL)),
            _resident((1, D_MODEL)),
            _resident((1, D_MODEL)),
            _resident((1, D_MODEL)),
            _resident((D_MODEL, D_FF)),
            _resident((D_FF, D_MODEL)),
            _resident((D_MODEL, D_MODEL)),
            _resident((PLE_DIM, D_MODEL)),
        ],
        out_specs=pl.BlockSpec((1, tm, D_MODEL), lambda b, t: (b, t, 0)),
        out_shape=jax.ShapeDtypeStruct((B, S, D_MODEL), f32),
        compiler_params=pltpu.CompilerParams(
            dimension_semantics=("arbitrary", "arbitrary"), vmem_limit_bytes=VMEM_LIMIT),
        name="post_mlp",
    )(x, p, ypool, oT, w_out[:POOL_WIDTH].astype(bf16), w_out[POOL_WIDTH:].astype(bf16),
      g_mix_post, g_mlp_pre, g_mlp_post, w_up.astype(bf16), w_down.astype(bf16),
      w_ple_gate.astype(bf16), w_ple_proj.astype(bf16))
    return out


def kernel(x, p, w_in, w_pool, pool_scale, w_out, rel_bias, g_mix_pre, g_mix_post,
           g_mlp_pre, g_mlp_post, w_up, w_down, w_ple_proj, w_ple_gate):
    depth = w_in.shape[0]
    h = x
    for i in range(depth):
        h = _layer(h, p[i], w_in[i], w_pool[i], pool_scale[i:i + 1], w_out[i], rel_bias,
                   g_mix_pre[i:i + 1], g_mix_post[i:i + 1], g_mlp_pre[i:i + 1],
                   g_mlp_post[i:i + 1], w_up[i], w_down[i], w_ple_proj[i], w_ple_gate[i])
    return h
```

```python
import math

import numpy as np
import jax
import jax.numpy as jnp
from jax import lax
from jax.experimental import pallas as pl
from jax.experimental.pallas import tpu as pltpu

D_MODEL = 1024
POOL_WIDTH = 512
POOL_WINDOWS = (2, 4, 8, 16)
POOL_CH = 128
POOL_HALO = 16
ATTN_WIDTH = 512
HEAD_DIM = 64
HEADS = 8
BLOCK = 256
TOP_K = 3
NUM_BUCKETS = 32
MAX_DISTANCE = 1024
D_FF = 4096
PLE_DIM = 256
EPS = 1e-6
NEG = -1e30
LOG2E = math.log2(math.e)
KAUG = 128
NEAR = 5
CHUNK = 4
CHUNK_KEYS = CHUNK * BLOCK
ACC_ROWS = HEAD_DIM + 16
SELECT_UNROLL = 4
LAG_PREV = 7
LAG_FAR = 12
LAG_MAX = max(CHUNK, LAG_PREV, LAG_FAR)

TM_IN = 1024
TM_POST = 512
VMEM_LIMIT = 56 * 1024 * 1024


def _bucket_thresholds():
    n = np.arange(0, NEAR * BLOCK + BLOCK, dtype=np.int64)
    max_exact = NUM_BUCKETS // 2
    nf = np.maximum(n, 1).astype(np.float64)
    large = max_exact + (np.log(nf / max_exact) / math.log(MAX_DISTANCE / max_exact)
                         * (NUM_BUCKETS - max_exact)).astype(np.int64)
    large = np.minimum(large, NUM_BUCKETS - 1)
    bucket = np.where(n < max_exact, n, large)
    assert np.all(np.diff(bucket) >= 0)
    return [int(np.argmax(bucket >= b)) for b in range(NUM_BUCKETS)], bucket


_THRESH, _BUCKET = _bucket_thresholds()
assert _BUCKET[(NEAR - 1) * BLOCK + 1] == NUM_BUCKETS - 1
assert NEAR <= CHUNK + 1


def _rms(x, g):
    ms = jnp.mean(x * x, axis=-1, keepdims=True)
    return x * lax.rsqrt(ms + EPS) * g


def _inproj_kernel(x_ref, g_ref, wp_ref, wq_ref, wk_ref, wv_ref, wpool_ref, pscale_ref,
                   ypool_ref, qT_ref, kaug_ref, kbar_ref, vT_ref, halo_scr):
    t = pl.program_id(1)
    tm = x_ref.shape[1]
    nblk = tm // BLOCK

    @pl.when(t == 0)
    def _():
        halo_scr[...] = jnp.zeros(halo_scr.shape, jnp.float32)

    a = _rms(x_ref[0], g_ref[...]).astype(jnp.bfloat16)

    zk = jnp.dot(a, wk_ref[...], preferred_element_type=jnp.float32)
    lane = lax.broadcasted_iota(jnp.int32, (tm, KAUG), 1)
    gblk = t * nblk + lax.broadcasted_iota(jnp.int32, (tm, KAUG), 0) // BLOCK
    onehot = jnp.where((lane == HEAD_DIM + gblk) | (lane == HEAD_DIM + 32 + gblk), 1.0, 0.0)
    for h in range(HEADS):
        pair = zk[:, (h // 2) * KAUG:(h // 2 + 1) * KAUG]
        if h % 2:
            pair = pltpu.roll(pair, HEAD_DIM, axis=1)
        kaug_ref[0, h] = jnp.where(lane < HEAD_DIM, pair, onehot).astype(kaug_ref.dtype)
    for bl in range(nblk):
        kbar_ref[0, bl] = jnp.mean(zk[bl * BLOCK:(bl + 1) * BLOCK, :], axis=0, keepdims=True)

    u = jnp.dot(a, wp_ref[...], preferred_element_type=jnp.float32)

    zq = jnp.dot(a, wq_ref[...], preferred_element_type=jnp.float32)
    for bl in range(nblk):
        qT_ref[0, bl] = zq[bl * BLOCK:(bl + 1) * BLOCK, :].T

    tpos = t * tm + lax.broadcasted_iota(jnp.int32, (tm, 1), 0)
    for g, w in enumerate(POOL_WINDOWS):
        cs = slice(g * POOL_CH, (g + 1) * POOL_CH)
        ug = u[:, cs]
        acc = jnp.concatenate([halo_scr[:, cs], ug], axis=0)
        span = 1
        while span < w:
            acc = acc + pltpu.roll(acc, span, axis=0)
            span *= 2
        cnt = jnp.minimum(tpos + 1, w).astype(jnp.float32)
        d = acc[POOL_HALO:] / cnt - ug
        y = jnp.dot(d.astype(jnp.bfloat16), wpool_ref[g], preferred_element_type=jnp.float32)
        ypool_ref[0, :, cs] = (y * pscale_ref[:, cs]).astype(ypool_ref.dtype)
    halo_scr[...] = u[tm - POOL_HALO:tm, :]

    zv = jnp.dot(a, wv_ref[...], preferred_element_type=jnp.float32)
    for bl in range(nblk):
        vT_ref[0, bl] = zv[bl * BLOCK:(bl + 1) * BLOCK, :].T.astype(vT_ref.dtype)


def _bias_tile(tab_ref, h, d):
    r = lax.broadcasted_iota(jnp.int32, (BLOCK, BLOCK), 0)
    c = lax.broadcasted_iota(jnp.int32, (BLOCK, BLOCK), 1)
    n = d * BLOCK + c - r
    lo, hi = (0 if d == 0 else (d - 1) * BLOCK + 1), d * BLOCK + BLOCK - 1
    b0 = int(_BUCKET[lo])
    tile = jnp.full((BLOCK, BLOCK), tab_ref[b0, h] * LOG2E, jnp.float32)
    for b in range(b0 + 1, NUM_BUCKETS):
        if _THRESH[b] > hi:
            break
        tile = jnp.where(n >= _THRESH[b], tab_ref[b, h] * LOG2E, tile)
    if d == 0:
        tile = jnp.where(n >= 0, tile, NEG)
    return tile


def _fold8(x):
    return x.reshape(x.shape[0] // 8, 8, x.shape[1])


def _attn_items(nb):
    own = [(i, i // CHUNK) for i in range(nb)]
    prev = [(i, i // CHUNK - 1) for i in range(CHUNK, nb)]
    far = [(i, c) for i in range(nb) for c in range(i // CHUNK - 1)]
    return tuple(np.asarray(v, np.int32).T.copy() for v in (own, prev, far))


def _attn_kernel(tab_ref, own_ref, prev_ref, far_ref, qT_ref, kaug_ref, vT_ref, kbar_ref, oT_ref,
                 qaug_scr, s_buf, p_buf, m_st, acc_st, bias_scr):
    h = pl.program_id(0)
    b = pl.program_id(1)
    nb = kbar_ref.shape[1]

    @pl.when(b == 0)
    def _():
        for d in range(bias_scr.shape[0]):
            if d < NEAR:
                bias_scr[d] = _bias_tile(tab_ref, h, d)
            else:
                bias_scr[d] = jnp.zeros((BLOCK, BLOCK), jnp.float32)

    kb2 = kbar_ref[0, :, 0, :]
    kb = jnp.where(h % 2 == 0, kb2[:, :HEAD_DIM], kb2[:, HEAD_DIM:])
    kb_hi = kb.astype(jnp.bfloat16)
    kb_lo = (kb - kb_hi.astype(jnp.float32)).astype(jnp.bfloat16)
    jidx = lax.broadcasted_iota(jnp.int32, (nb, BLOCK), 0)
    cfar = jnp.full((nb, BLOCK), tab_ref[NUM_BUCKETS - 1, h] * LOG2E, jnp.float32)
    cfar_hi = cfar.astype(jnp.bfloat16).astype(jnp.float32)
    cfar_lo = cfar - cfar_hi

    def select(i):
        qT = qT_ref[0, i]
        q_hi = qT.astype(jnp.bfloat16)
        q_lo = (qT - q_hi.astype(jnp.float32)).astype(jnp.bfloat16)
        sc = (jnp.dot(kb_hi, q_hi, preferred_element_type=jnp.float32)
              + (jnp.dot(kb_hi, q_lo, preferred_element_type=jnp.float32)
                 + jnp.dot(kb_lo, q_hi, preferred_element_type=jnp.float32)))
        past = jidx < i
        work = jnp.where(past, sc, NEG)
        chosen = jnp.zeros((nb, BLOCK), jnp.bool_)
        for _ in range(TOP_K):
            mx = jnp.max(work, axis=0, keepdims=True)
            first = jnp.min(jnp.where(work == mx, jidx, nb), axis=0, keepdims=True)
            pick = jidx == first
            chosen = chosen | pick
            work = jnp.where(pick, -jnp.inf, work)
        sel = chosen & past
        far = jidx <= i - NEAR
        code_hi = jnp.where(past, jnp.where(sel, jnp.where(far, cfar_hi, 0.0), NEG),
                            jnp.where(jidx == i, 0.0, NEG))
        code_lo = jnp.where(sel & far, cfar_lo, 0.0)
        qaug_scr[i] = jnp.concatenate(
            [(qT * (HEAD_DIM ** -0.5 * LOG2E)).astype(jnp.bfloat16),
             code_hi.astype(jnp.bfloat16), code_lo.astype(jnp.bfloat16)], axis=0)

    def select_group(g, carry):
        for u in range(SELECT_UNROLL):
            select(g * SELECT_UNROLL + u)
        return carry

    lax.fori_loop(0, nb // SELECT_UNROLL, select_group, 0)

    def stage_qk(item, slot, near, nblk):
        i, c = item
        q_aug = qaug_scr[i]
        cm = None
        for w in range(nblk):
            row0 = pl.multiple_of((c * CHUNK + w) * BLOCK, BLOCK)
            sw = jnp.dot(kaug_ref[0, 0, pl.ds(row0, BLOCK), :], q_aug,
                         preferred_element_type=jnp.float32)
            if near:
                sw = sw + bias_scr[i - (c * CHUNK + w)]
            s_buf[slot, w] = sw
            fm = jnp.max(_fold8(sw), axis=0)
            cm = fm if cm is None else jnp.maximum(cm, fm)
        return cm

    def stage_exp(item, slot, cm, nblk, first):
        i, _ = item
        col_max = jnp.max(cm, axis=0, keepdims=True)
        if first:
            m_new = jnp.broadcast_to(col_max, cm.shape)
            alpha = None
        else:
            m_old = m_st[i]
            m_new = jnp.maximum(m_old, col_max)
            alpha = jnp.exp2(m_old - m_new)[0:1]
        m_st[i] = m_new
        m_row = m_new[0:1]
        for w in range(nblk):
            p = jnp.exp2(s_buf[slot, w] - m_row)
            p_buf[slot, w * BLOCK:(w + 1) * BLOCK, :] = p.astype(jnp.bfloat16)
        return alpha

    def ones_rows(keys):
        first = lax.broadcasted_iota(jnp.int32, (ACC_ROWS - HEAD_DIM, keys), 0) == 0
        return first.astype(jnp.bfloat16)

    def stage_pv(item, slot, alpha, nblk):
        i, c = item
        vblk = jnp.concatenate([vT_ref[0, c * CHUNK + w] for w in range(nblk)], axis=1)
        vblk = jnp.concatenate([vblk, ones_rows(nblk * BLOCK)], axis=0)
        pv = jnp.dot(vblk, p_buf[slot, :nblk * BLOCK, :],
                     preferred_element_type=jnp.float32)
        acc_st[i] = pv if alpha is None else alpha * acc_st[i] + pv

    def run(items_ref, near, lag, nblk_of_slot=lambda j: CHUNK, first=False):
        n = items_ref.shape[1]
        assert n % lag == 0 and n >= 2 * lag and lag <= s_buf.shape[0]
        item = lambda x: (items_ref[0, x], items_ref[1, x])

        def softmax_pv(x, j, cm):
            alpha = stage_exp(item(x), j, cm, nblk_of_slot(j), first)
            stage_pv(item(x), j, alpha, nblk_of_slot(j))

        cms = tuple(stage_qk(item(x), x, near, nblk_of_slot(x)) for x in range(lag))

        def body(k, cms):
            t = lag * k
            new = []
            for j in range(lag):
                softmax_pv(t - lag + j, j, cms[j])
                new.append(stage_qk(item(t + j), j, near, nblk_of_slot(j)))
            return tuple(new)

        cms = lax.fori_loop(1, n // lag, body, cms)
        for j in range(lag):
            softmax_pv(n - lag + j, j, cms[j])

    run(own_ref, True, CHUNK, lambda j: j + 1, first=True)
    run(prev_ref, True, LAG_PREV)
    run(far_ref, False, LAG_FAR)

    def finish(i, carry):
        acc = acc_st[i]
        oT_ref[0, i] = (acc[:HEAD_DIM] / acc[HEAD_DIM:HEAD_DIM + 1]).astype(oT_ref.dtype)
        return carry

    lax.fori_loop(0, nb, finish, 0)


def _post_kernel(x_ref, p_ref, ypool_ref, oT_ref, wop_ref, woa_ref, gpost_ref, gpre_ref,
                 gmpost_ref, wup_ref, wdown_ref, wgate_ref, wproj_ref, out_ref):
    nblk = oT_ref.shape[1]
    yat = jnp.concatenate([oT_ref[0, bl].T for bl in range(nblk)], axis=0)
    mix = (jnp.dot(ypool_ref[0], wop_ref[...], preferred_element_type=jnp.float32)
           + jnp.dot(yat.astype(jnp.bfloat16), woa_ref[...], preferred_element_type=jnp.float32))
    h1 = x_ref[0] + _rms(mix, gpost_ref[...])
    m = _rms(h1, gpre_ref[...]).astype(jnp.bfloat16)
    up = jnp.dot(m, wup_ref[...], preferred_element_type=jnp.float32)
    act = jnp.square(jnp.maximum(up, 0.0)).astype(jnp.bfloat16)
    f = jnp.dot(act, wdown_ref[...], preferred_element_type=jnp.float32)
    h2 = h1 + _rms(f, gmpost_ref[...])
    gate = jax.nn.sigmoid(jnp.dot(h2.astype(jnp.bfloat16), wgate_ref[...],
                                  preferred_element_type=jnp.float32))
    pe = jnp.dot(p_ref[0].astype(jnp.bfloat16), wproj_ref[...], preferred_element_type=jnp.float32)
    out_ref[0] = h2 + gate * pe


def _resident(shape):
    return pl.BlockSpec(shape, lambda *_: (0,) * len(shape), pipeline_mode=pl.Buffered(1))


def _mixers(x, w_in, w_pool, pool_scale, rel_bias, g_mix_pre):
    B, S, _ = x.shape
    nb = S // BLOCK
    bf16 = jnp.bfloat16
    f32 = jnp.float32

    wp = w_in[:, :POOL_WIDTH].astype(bf16)
    wq = w_in[:, POOL_WIDTH:POOL_WIDTH + ATTN_WIDTH].astype(bf16)
    wk = w_in[:, POOL_WIDTH + ATTN_WIDTH:POOL_WIDTH + 2 * ATTN_WIDTH].astype(bf16)
    wv = w_in[:, POOL_WIDTH + 2 * ATTN_WIDTH:].astype(bf16)

    tm = TM_IN
    nblk = tm // BLOCK
    ypool, qT, kaug, kbar, vT = pl.pallas_call(
        _inproj_kernel,
        grid=(B, S // tm),
        in_specs=[
            pl.BlockSpec((1, tm, D_MODEL), lambda b, t: (b, t, 0)),
            _resident((1, D_MODEL)),
            _resident((D_MODEL, POOL_WIDTH)),
            _resident((D_MODEL, ATTN_WIDTH)),
            _resident((D_MODEL, ATTN_WIDTH)),
            _resident((D_MODEL, ATTN_WIDTH)),
            _resident((len(POOL_WINDOWS), POOL_CH, POOL_CH)),
            _resident((1, POOL_WIDTH)),
        ],
        out_specs=[
            pl.BlockSpec((1, tm, POOL_WIDTH), lambda b, t: (b, t, 0)),
            pl.BlockSpec((1, nblk, ATTN_WIDTH, BLOCK), lambda b, t: (b, t, 0, 0)),
            pl.BlockSpec((1, HEADS, tm, KAUG), lambda b, t: (b, 0, t, 0)),
            pl.BlockSpec((1, nblk, 1, ATTN_WIDTH), lambda b, t: (b, t, 0, 0)),
            pl.BlockSpec((1, nblk, ATTN_WIDTH, BLOCK), lambda b, t: (b, t, 0, 0)),
        ],
        out_shape=[
            jax.ShapeDtypeStruct((B, S, POOL_WIDTH), bf16),
            jax.ShapeDtypeStruct((B, nb, ATTN_WIDTH, BLOCK), f32),
            jax.ShapeDtypeStruct((B, HEADS, S, KAUG), bf16),
            jax.ShapeDtypeStruct((B, nb, 1, ATTN_WIDTH), f32),
            jax.ShapeDtypeStruct((B, nb, ATTN_WIDTH, BLOCK), bf16),
        ],
        scratch_shapes=[pltpu.VMEM((POOL_HALO, POOL_WIDTH), f32)],
        compiler_params=pltpu.CompilerParams(
            dimension_semantics=("arbitrary", "arbitrary"), vmem_limit_bytes=VMEM_LIMIT),
        name="inproj_pool",
    )(x, g_mix_pre, wp, wq, wk, wv, w_pool.astype(bf16), pool_scale)

    own_items, prev_items, far_items = _attn_items(nb)
    smem = pl.BlockSpec(memory_space=pltpu.SMEM)
    oT = pl.pallas_call(
        _attn_kernel,
        grid=(HEADS, B),
        in_specs=[
            smem, smem, smem, smem,
            pl.BlockSpec((1, nb, HEAD_DIM, BLOCK), lambda h, b: (b, 0, h, 0)),
            pl.BlockSpec((1, 1, S, KAUG), lambda h, b: (b, h, 0, 0)),
            pl.BlockSpec((1, nb, HEAD_DIM, BLOCK), lambda h, b: (b, 0, h, 0)),
            pl.BlockSpec((1, nb, 1, 2 * HEAD_DIM), lambda h, b: (b, 0, 0, h // 2)),
        ],
        out_specs=pl.BlockSpec((1, nb, HEAD_DIM, BLOCK), lambda h, b: (b, 0, h, 0)),
        out_shape=jax.ShapeDtypeStruct((B, nb, ATTN_WIDTH, BLOCK), f32),
        scratch_shapes=[
            pltpu.VMEM((nb, KAUG, BLOCK), bf16),
            pltpu.VMEM((LAG_MAX, CHUNK, BLOCK, BLOCK), f32),
            pltpu.VMEM((LAG_MAX, CHUNK_KEYS, BLOCK), bf16),
            pltpu.VMEM((nb, 8, BLOCK), f32),
            pltpu.VMEM((nb, ACC_ROWS, BLOCK), f32),
            pltpu.VMEM((2 * CHUNK, BLOCK, BLOCK), f32),
        ],
        compiler_params=pltpu.CompilerParams(
            dimension_semantics=("arbitrary", "arbitrary"), vmem_limit_bytes=VMEM_LIMIT),
        name="moba_attention",
    )(rel_bias, jnp.asarray(own_items), jnp.asarray(prev_items), jnp.asarray(far_items),
      qT, kaug, vT, kbar)
    return ypool, oT


def _layer(x, p, w_in, w_pool, pool_scale, w_out, rel_bias, g_mix_pre, g_mix_post,
           g_mlp_pre, g_mlp_post, w_up, w_down, w_ple_proj, w_ple_gate):
    B, S, _ = x.shape
    bf16 = jnp.bfloat16
    f32 = jnp.float32
    ypool, oT = _mixers(x, w_in, w_pool, pool_scale, rel_bias, g_mix_pre)

    tm = TM_POST
    nblk = tm // BLOCK
    out = pl.pallas_call(
        _post_kernel,
        grid=(B, S // tm),
        in_specs=[
            pl.BlockSpec((1, tm, D_MODEL), lambda b, t: (b, t, 0)),
            pl.BlockSpec((1, tm, PLE_DIM), lambda b, t: (b, t, 0)),
            pl.BlockSpec((1, tm, POOL_WIDTH), lambda b, t: (b, t, 0)),
            pl.BlockSpec((1, nblk, ATTN_WIDTH, BLOCK), lambda b, t: (b, t, 0, 0)),
            _resident((POOL_WIDTH, D_MODEL)),
            _resident((ATTN_WIDTH, D_MODEL)),
            _resident((1, D_MODEL)),
            _resident((1, D_MODEL)),
            _resident((1, D_MODEL)),
            _resident((D_MODEL, D_FF)),
            _resident((D_FF, D_MODEL)),
            _resident((D_MODEL, D_MODEL)),
            _resident((PLE_DIM, D_MODEL)),
        ],
        out_specs=pl.BlockSpec((1, tm, D_MODEL), lambda b, t: (b, t, 0)),
        out_shape=jax.ShapeDtypeStruct((B, S, D_MODEL), f32),
        compiler_params=pltpu.CompilerParams(
            dimension_semantics=("arbitrary", "arbitrary"), vmem_limit_bytes=VMEM_LIMIT),
        name="post_mlp",
    )(x, p, ypool, oT, w_out[:POOL_WIDTH].astype(bf16), w_out[POOL_WIDTH:].astype(bf16),
      g_mix_post, g_mlp_pre, g_mlp_post, w_up.astype(bf16), w_down.astype(bf16),
      w_ple_gate.astype(bf16), w_ple_proj.astype(bf16))
    return out


def kernel(x, p, w_in, w_pool, pool_scale, w_out, rel_bias, g_mix_pre, g_mix_post,
           g_mlp_pre, g_mlp_post, w_up, w_down, w_ple_proj, w_ple_gate):
    depth = w_in.shape[0]
    h = x
    for i in range(depth):
        h = _layer(h, p[i], w_in[i], w_pool[i], pool_scale[i:i + 1], w_out[i], rel_bias,
                   g_mix_pre[i:i + 1], g_mix_post[i:i + 1], g_mlp_pre[i:i + 1],
                   g_mlp_post[i:i + 1], w_up[i], w_down[i], w_ple_proj[i], w_ple_gate[i])
    return h
```

```python
import functools
import math

import numpy as np
import jax
import jax.numpy as jnp
from jax import lax
from jax.experimental import pallas as pl
from jax.experimental.pallas import tpu as pltpu

D_MODEL = 1024
POOL_WIDTH = 512
POOL_WINDOWS = (2, 4, 8, 16)
POOL_CH = 128
POOL_HALO = 16
ATTN_WIDTH = 512
HEAD_DIM = 64
HEADS = 8
BLOCK = 256
TOP_K = 3
NUM_BUCKETS = 32
MAX_DISTANCE = 1024
D_FF = 4096
PLE_DIM = 256
EPS = 1e-6
NEG = -1e30
LOG2E = math.log2(math.e)
KAUG = 128
NEAR = 5
CHUNK = 4
CHUNK_KEYS = CHUNK * BLOCK
ACC_ROWS = HEAD_DIM + 16
SELECT_UNROLL = 4
LAG_PREV = 7
LAG_FAR = 12
LAG_MAX = max(CHUNK, LAG_PREV, LAG_FAR)

TM_IN = 1024
TM_POST = 512
VMEM_LIMIT = 56 * 1024 * 1024


def _bucket_thresholds():
    n = np.arange(0, NEAR * BLOCK + BLOCK, dtype=np.int64)
    max_exact = NUM_BUCKETS // 2
    nf = np.maximum(n, 1).astype(np.float64)
    large = max_exact + (np.log(nf / max_exact) / math.log(MAX_DISTANCE / max_exact)
                         * (NUM_BUCKETS - max_exact)).astype(np.int64)
    large = np.minimum(large, NUM_BUCKETS - 1)
    bucket = np.where(n < max_exact, n, large)
    assert np.all(np.diff(bucket) >= 0)
    return [int(np.argmax(bucket >= b)) for b in range(NUM_BUCKETS)], bucket


_THRESH, _BUCKET = _bucket_thresholds()
assert _BUCKET[(NEAR - 1) * BLOCK + 1] == NUM_BUCKETS - 1
assert NEAR <= CHUNK + 1
assert POOL_WIDTH == ATTN_WIDTH


def _rms(x, g):
    ms = jnp.mean(x * x, axis=-1, keepdims=True)
    return x * lax.rsqrt(ms + EPS) * g


def _inproj_kernel(n_cast, x_ref, g_ref, wp_ref, wq_ref, wk_ref, wv_ref, wpool_ref, pscale_ref,
                   *refs):
    cast_in, refs = refs[:n_cast], refs[n_cast:]
    ypool_ref, qT_ref, kaug_ref, kbar_ref, vT_ref = refs[:5]
    cast_out, (halo_scr,) = refs[5:5 + n_cast], refs[5 + n_cast:]
    t = pl.program_id(1)

    for src, dst in zip(cast_in, cast_out):
        dst[...] = src[...].astype(dst.dtype)
    tm = x_ref.shape[1]
    nblk = tm // BLOCK

    @pl.when(t == 0)
    def _():
        halo_scr[...] = jnp.zeros(halo_scr.shape, jnp.float32)

    a = _rms(x_ref[0], g_ref[...]).astype(jnp.bfloat16)

    zk = jnp.dot(a, wk_ref[...], preferred_element_type=jnp.float32)
    lane = lax.broadcasted_iota(jnp.int32, (tm, KAUG), 1)
    gblk = t * nblk + lax.broadcasted_iota(jnp.int32, (tm, KAUG), 0) // BLOCK
    onehot = jnp.where((lane == HEAD_DIM + gblk) | (lane == HEAD_DIM + 32 + gblk), 1.0, 0.0)
    for h in range(HEADS):
        pair = zk[:, (h // 2) * KAUG:(h // 2 + 1) * KAUG]
        if h % 2:
            pair = pltpu.roll(pair, HEAD_DIM, axis=1)
        kaug_ref[0, h] = jnp.where(lane < HEAD_DIM, pair, onehot).astype(kaug_ref.dtype)
    for bl in range(nblk):
        kbar_ref[0, bl] = jnp.mean(zk[bl * BLOCK:(bl + 1) * BLOCK, :], axis=0, keepdims=True)

    u = jnp.dot(a, wp_ref[...], preferred_element_type=jnp.float32)

    zq = jnp.dot(a, wq_ref[...], preferred_element_type=jnp.float32)
    for bl in range(nblk):
        qT_ref[0, bl] = zq[bl * BLOCK:(bl + 1) * BLOCK, :].T

    tpos = t * tm + lax.broadcasted_iota(jnp.int32, (tm, 1), 0)
    for g, w in enumerate(POOL_WINDOWS):
        cs = slice(g * POOL_CH, (g + 1) * POOL_CH)
        ug = u[:, cs]
        acc = jnp.concatenate([halo_scr[:, cs], ug], axis=0)
        span = 1
        while span < w:
            acc = acc + pltpu.roll(acc, span, axis=0)
            span *= 2
        cnt = jnp.minimum(tpos + 1, w).astype(jnp.float32)
        d = acc[POOL_HALO:] / cnt - ug
        y = jnp.dot(d.astype(jnp.bfloat16), wpool_ref[g], preferred_element_type=jnp.float32)
        ypool_ref[0, :, cs] = (y * pscale_ref[:, cs]).astype(ypool_ref.dtype)
    halo_scr[...] = u[tm - POOL_HALO:tm, :]

    zv = jnp.dot(a, wv_ref[...], preferred_element_type=jnp.float32)
    for bl in range(nblk):
        vT_ref[0, bl] = zv[bl * BLOCK:(bl + 1) * BLOCK, :].T.astype(vT_ref.dtype)


def _bias_tile(tab_ref, h, d):
    r = lax.broadcasted_iota(jnp.int32, (BLOCK, BLOCK), 0)
    c = lax.broadcasted_iota(jnp.int32, (BLOCK, BLOCK), 1)
    n = d * BLOCK + c - r
    lo, hi = (0 if d == 0 else (d - 1) * BLOCK + 1), d * BLOCK + BLOCK - 1
    b0 = int(_BUCKET[lo])
    tile = jnp.full((BLOCK, BLOCK), tab_ref[b0, h] * LOG2E, jnp.float32)
    for b in range(b0 + 1, NUM_BUCKETS):
        if _THRESH[b] > hi:
            break
        tile = jnp.where(n >= _THRESH[b], tab_ref[b, h] * LOG2E, tile)
    if d == 0:
        tile = jnp.where(n >= 0, tile, NEG)
    return tile


def _fold8(x):
    return x.reshape(x.shape[0] // 8, 8, x.shape[1])


def _attn_items(nb):
    own = [(i, i // CHUNK) for i in range(nb)]
    prev = [(i, i // CHUNK - 1) for i in range(CHUNK, nb)]
    far = [(i, c) for i in range(nb) for c in range(i // CHUNK - 1)]
    return tuple(np.asarray(v, np.int32).T.copy() for v in (own, prev, far))


def _attn_kernel(tab_ref, own_ref, prev_ref, far_ref, qT_ref, kaug_ref, vT_ref, kbar_ref, oT_ref,
                 qaug_scr, s_buf, p_buf, m_st, acc_st, bias_scr):
    h = pl.program_id(0)
    b = pl.program_id(1)
    nb = kbar_ref.shape[1]

    @pl.when(b == 0)
    def _():
        for d in range(bias_scr.shape[0]):
            if d < NEAR:
                bias_scr[d] = _bias_tile(tab_ref, h, d)
            else:
                bias_scr[d] = jnp.zeros((BLOCK, BLOCK), jnp.float32)

    kb2 = kbar_ref[0, :, 0, :]
    kb = jnp.where(h % 2 == 0, kb2[:, :HEAD_DIM], kb2[:, HEAD_DIM:])
    kb_hi = kb.astype(jnp.bfloat16)
    kb_lo = (kb - kb_hi.astype(jnp.float32)).astype(jnp.bfloat16)
    jidx = lax.broadcasted_iota(jnp.int32, (nb, BLOCK), 0)
    cfar = jnp.full((nb, BLOCK), tab_ref[NUM_BUCKETS - 1, h] * LOG2E, jnp.float32)
    cfar_hi = cfar.astype(jnp.bfloat16).astype(jnp.float32)
    cfar_lo = cfar - cfar_hi

    def select(i):
        qT = qT_ref[0, i]
        q_hi = qT.astype(jnp.bfloat16)
        q_lo = (qT - q_hi.astype(jnp.float32)).astype(jnp.bfloat16)
        sc = (jnp.dot(kb_hi, q_hi, preferred_element_type=jnp.float32)
              + (jnp.dot(kb_hi, q_lo, preferred_element_type=jnp.float32)
                 + jnp.dot(kb_lo, q_hi, preferred_element_type=jnp.float32)))
        past = jidx < i
        work = jnp.where(past, sc, NEG)
        chosen = jnp.zeros((nb, BLOCK), jnp.bool_)
        for _ in range(TOP_K):
            mx = jnp.max(work, axis=0, keepdims=True)
            first = jnp.min(jnp.where(work == mx, jidx, nb), axis=0, keepdims=True)
            pick = jidx == first
            chosen = chosen | pick
            work = jnp.where(pick, -jnp.inf, work)
        sel = chosen & past
        far = jidx <= i - NEAR
        code_hi = jnp.where(past, jnp.where(sel, jnp.where(far, cfar_hi, 0.0), NEG),
                            jnp.where(jidx == i, 0.0, NEG))
        code_lo = jnp.where(sel & far, cfar_lo, 0.0)
        qaug_scr[i] = jnp.concatenate(
            [(qT * (HEAD_DIM ** -0.5 * LOG2E)).astype(jnp.bfloat16),
             code_hi.astype(jnp.bfloat16), code_lo.astype(jnp.bfloat16)], axis=0)

    def select_group(g, carry):
        for u in range(SELECT_UNROLL):
            select(g * SELECT_UNROLL + u)
        return carry

    lax.fori_loop(0, nb // SELECT_UNROLL, select_group, 0)

    def stage_qk(item, slot, near, nblk):
        i, c = item
        q_aug = qaug_scr[i]
        cm = None
        for w in range(nblk):
            row0 = pl.multiple_of((c * CHUNK + w) * BLOCK, BLOCK)
            sw = jnp.dot(kaug_ref[0, 0, pl.ds(row0, BLOCK), :], q_aug,
                         preferred_element_type=jnp.float32)
            if near:
                sw = sw + bias_scr[i - (c * CHUNK + w)]
            s_buf[slot, w] = sw
            fm = jnp.max(_fold8(sw), axis=0)
            cm = fm if cm is None else jnp.maximum(cm, fm)
        return cm

    def stage_exp(item, slot, cm, nblk, first):
        i, _ = item
        col_max = jnp.max(cm, axis=0, keepdims=True)
        if first:
            m_new = jnp.broadcast_to(col_max, cm.shape)
            alpha = None
        else:
            m_old = m_st[i]
            m_new = jnp.maximum(m_old, col_max)
            alpha = jnp.exp2(m_old - m_new)[0:1]
        m_st[i] = m_new
        m_row = m_new[0:1]
        for w in range(nblk):
            p = jnp.exp2(s_buf[slot, w] - m_row)
            p_buf[slot, w * BLOCK:(w + 1) * BLOCK, :] = p.astype(jnp.bfloat16)
        return alpha

    def ones_rows(keys):
        first = lax.broadcasted_iota(jnp.int32, (ACC_ROWS - HEAD_DIM, keys), 0) == 0
        return first.astype(jnp.bfloat16)

    def stage_pv(item, slot, alpha, nblk):
        i, c = item
        vblk = jnp.concatenate([vT_ref[0, c * CHUNK + w] for w in range(nblk)], axis=1)
        vblk = jnp.concatenate([vblk, ones_rows(nblk * BLOCK)], axis=0)
        pv = jnp.dot(vblk, p_buf[slot, :nblk * BLOCK, :],
                     preferred_element_type=jnp.float32)
        acc_st[i] = pv if alpha is None else alpha * acc_st[i] + pv

    def run(items_ref, near, lag, nblk_of_slot=lambda j: CHUNK, first=False):
        n = items_ref.shape[1]
        assert n % lag == 0 and n >= 2 * lag and lag <= s_buf.shape[0]
        item = lambda x: (items_ref[0, x], items_ref[1, x])

        def softmax_pv(x, j, cm):
            alpha = stage_exp(item(x), j, cm, nblk_of_slot(j), first)
            stage_pv(item(x), j, alpha, nblk_of_slot(j))

        cms = tuple(stage_qk(item(x), x, near, nblk_of_slot(x)) for x in range(lag))

        def body(k, cms):
            t = lag * k
            new = []
            for j in range(lag):
                softmax_pv(t - lag + j, j, cms[j])
                new.append(stage_qk(item(t + j), j, near, nblk_of_slot(j)))
            return tuple(new)

        cms = lax.fori_loop(1, n // lag, body, cms)
        for j in range(lag):
            softmax_pv(n - lag + j, j, cms[j])

    run(own_ref, True, CHUNK, lambda j: j + 1, first=True)
    run(prev_ref, True, LAG_PREV)
    run(far_ref, False, LAG_FAR)

    def finish(i, carry):
        acc = acc_st[i]
        oT_ref[0, i] = (acc[:HEAD_DIM] / acc[HEAD_DIM:HEAD_DIM + 1]).astype(oT_ref.dtype)
        return carry

    lax.fori_loop(0, nb, finish, 0)


def _post_kernel(x_ref, p_ref, ypool_ref, oT_ref, wop_ref, woa_ref, gpost_ref, gpre_ref,
                 gmpost_ref, wup_ref, wdown_ref, wgate_ref, wproj_ref, out_ref):
    nblk = oT_ref.shape[1]
    yat = jnp.concatenate([oT_ref[0, bl].T for bl in range(nblk)], axis=0)
    mix = (jnp.dot(ypool_ref[0], wop_ref[...], preferred_element_type=jnp.float32)
           + jnp.dot(yat.astype(jnp.bfloat16), woa_ref[...], preferred_element_type=jnp.float32))
    h1 = x_ref[0] + _rms(mix, gpost_ref[...])
    m = _rms(h1, gpre_ref[...]).astype(jnp.bfloat16)
    up = jnp.dot(m, wup_ref[...], preferred_element_type=jnp.float32)
    act = jnp.square(jnp.maximum(up, 0.0)).astype(jnp.bfloat16)
    f = jnp.dot(act, wdown_ref[...], preferred_element_type=jnp.float32)
    h2 = h1 + _rms(f, gmpost_ref[...])
    gate = jax.nn.sigmoid(jnp.dot(h2.astype(jnp.bfloat16), wgate_ref[...],
                                  preferred_element_type=jnp.float32))
    pe = jnp.dot(p_ref[0].astype(jnp.bfloat16), wproj_ref[...], preferred_element_type=jnp.float32)
    out_ref[0] = h2 + gate * pe


def _resident(shape):
    return pl.BlockSpec(shape, lambda *_: (0,) * len(shape), pipeline_mode=pl.Buffered(1))


def _mixers(x, w_in, w_pool, pool_scale, rel_bias, g_mix_pre, later_weights=()):
    B, S, _ = x.shape
    nb = S // BLOCK
    bf16 = jnp.bfloat16
    f32 = jnp.float32

    wp = w_in[:, :POOL_WIDTH].astype(bf16)
    wq = w_in[:, POOL_WIDTH:POOL_WIDTH + ATTN_WIDTH].astype(bf16)
    wk = w_in[:, POOL_WIDTH + ATTN_WIDTH:POOL_WIDTH + 2 * ATTN_WIDTH].astype(bf16)
    wv = w_in[:, POOL_WIDTH + 2 * ATTN_WIDTH:].astype(bf16)

    tm = TM_IN
    nblk = tm // BLOCK
    steps_b = S // tm
    slab = lambda w: pl.BlockSpec((w.shape[0] // (B * steps_b), w.shape[1]),
                                  lambda b, t: (b * steps_b + t, 0))
    ypool, qT, kaug, kbar, vT, *cast = pl.pallas_call(
        functools.partial(_inproj_kernel, len(later_weights)),
        grid=(B, steps_b),
        in_specs=[
            pl.BlockSpec((1, tm, D_MODEL), lambda b, t: (b, t, 0)),
            _resident((1, D_MODEL)),
            _resident((D_MODEL, POOL_WIDTH)),
            _resident((D_MODEL, ATTN_WIDTH)),
            _resident((D_MODEL, ATTN_WIDTH)),
            _resident((D_MODEL, ATTN_WIDTH)),
            _resident((len(POOL_WINDOWS), POOL_CH, POOL_CH)),
            _resident((1, POOL_WIDTH)),
            *[slab(w) for w in later_weights],
        ],
        out_specs=[
            pl.BlockSpec((1, tm, POOL_WIDTH), lambda b, t: (b, t, 0)),
            pl.BlockSpec((1, nblk, ATTN_WIDTH, BLOCK), lambda b, t: (b, t, 0, 0)),
            pl.BlockSpec((1, HEADS, tm, KAUG), lambda b, t: (b, 0, t, 0)),
            pl.BlockSpec((1, nblk, 1, ATTN_WIDTH), lambda b, t: (b, t, 0, 0)),
            pl.BlockSpec((1, nblk, ATTN_WIDTH, BLOCK), lambda b, t: (b, t, 0, 0)),
            *[slab(w) for w in later_weights],
        ],
        out_shape=[
            jax.ShapeDtypeStruct((B, S, POOL_WIDTH), bf16),
            jax.ShapeDtypeStruct((B, nb, ATTN_WIDTH, BLOCK), f32),
            jax.ShapeDtypeStruct((B, HEADS, S, KAUG), bf16),
            jax.ShapeDtypeStruct((B, nb, 1, ATTN_WIDTH), f32),
            jax.ShapeDtypeStruct((B, nb, ATTN_WIDTH, BLOCK), bf16),
            *[jax.ShapeDtypeStruct(w.shape, bf16) for w in later_weights],
        ],
        scratch_shapes=[pltpu.VMEM((POOL_HALO, POOL_WIDTH), f32)],
        compiler_params=pltpu.CompilerParams(
            dimension_semantics=("arbitrary", "arbitrary"), vmem_limit_bytes=VMEM_LIMIT),
        name="inproj_pool",
    )(x, g_mix_pre, wp, wq, wk, wv, w_pool.astype(bf16), pool_scale, *later_weights)

    own_items, prev_items, far_items = _attn_items(nb)
    smem = pl.BlockSpec(memory_space=pltpu.SMEM)
    oT = pl.pallas_call(
        _attn_kernel,
        grid=(HEADS, B),
        in_specs=[
            smem, smem, smem, smem,
            pl.BlockSpec((1, nb, HEAD_DIM, BLOCK), lambda h, b: (b, 0, h, 0)),
            pl.BlockSpec((1, 1, S, KAUG), lambda h, b: (b, h, 0, 0)),
            pl.BlockSpec((1, nb, HEAD_DIM, BLOCK), lambda h, b: (b, 0, h, 0)),
            pl.BlockSpec((1, nb, 1, 2 * HEAD_DIM), lambda h, b: (b, 0, 0, h // 2)),
        ],
        out_specs=pl.BlockSpec((1, nb, HEAD_DIM, BLOCK), lambda h, b: (b, 0, h, 0)),
        out_shape=jax.ShapeDtypeStruct((B, nb, ATTN_WIDTH, BLOCK), f32),
        scratch_shapes=[
            pltpu.VMEM((nb, KAUG, BLOCK), bf16),
            pltpu.VMEM((LAG_MAX, CHUNK, BLOCK, BLOCK), f32),
            pltpu.VMEM((LAG_MAX, CHUNK_KEYS, BLOCK), bf16),
            pltpu.VMEM((nb, 8, BLOCK), f32),
            pltpu.VMEM((nb, ACC_ROWS, BLOCK), f32),
            pltpu.VMEM((2 * CHUNK, BLOCK, BLOCK), f32),
        ],
        compiler_params=pltpu.CompilerParams(
            dimension_semantics=("arbitrary", "arbitrary"), vmem_limit_bytes=VMEM_LIMIT),
        name="moba_attention",
    )(rel_bias, jnp.asarray(own_items), jnp.asarray(prev_items), jnp.asarray(far_items),
      qT, kaug, vT, kbar)
    return ypool, oT, cast


def _layer(x, p, w_in, w_pool, pool_scale, w_out, rel_bias, g_mix_pre, g_mix_post,
           g_mlp_pre, g_mlp_post, w_up, w_down, w_ple_proj, w_ple_gate):
    B, S, _ = x.shape
    bf16 = jnp.bfloat16
    f32 = jnp.float32
    ypool, oT, (wo, wup, wdown, wgate, wproj) = _mixers(
        x, w_in, w_pool, pool_scale, rel_bias, g_mix_pre,
        later_weights=(w_out, w_up, w_down, w_ple_gate, w_ple_proj))

    tm = TM_POST
    nblk = tm // BLOCK
    out = pl.pallas_call(
        _post_kernel,
        grid=(B, S // tm),
        in_specs=[
            pl.BlockSpec((1, tm, D_MODEL), lambda b, t: (b, t, 0)),
            pl.BlockSpec((1, tm, PLE_DIM), lambda b, t: (b, t, 0)),
            pl.BlockSpec((1, tm, POOL_WIDTH), lambda b, t: (b, t, 0)),
            pl.BlockSpec((1, nblk, ATTN_WIDTH, BLOCK), lambda b, t: (b, t, 0, 0)),
            pl.BlockSpec((POOL_WIDTH, D_MODEL), lambda b, t: (0, 0), pipeline_mode=pl.Buffered(1)),
            pl.BlockSpec((ATTN_WIDTH, D_MODEL), lambda b, t: (1, 0), pipeline_mode=pl.Buffered(1)),
            _resident((1, D_MODEL)),
            _resident((1, D_MODEL)),
            _resident((1, D_MODEL)),
            _resident((D_MODEL, D_FF)),
            _resident((D_FF, D_MODEL)),
            _resident((D_MODEL, D_MODEL)),
            _resident((PLE_DIM, D_MODEL)),
        ],
        out_specs=pl.BlockSpec((1, tm, D_MODEL), lambda b, t: (b, t, 0)),
        out_shape=jax.ShapeDtypeStruct((B, S, D_MODEL), f32),
        compiler_params=pltpu.CompilerParams(
            dimension_semantics=("arbitrary", "arbitrary"), vmem_limit_bytes=VMEM_LIMIT),
        name="post_mlp",
    )(x, p, ypool, oT, wo, wo, g_mix_post, g_mlp_pre, g_mlp_post, wup, wdown, wgate, wproj)
    return out


def kernel(x, p, w_in, w_pool, pool_scale, w_out, rel_bias, g_mix_pre, g_mix_post,
           g_mlp_pre, g_mlp_post, w_up, w_down, w_ple_proj, w_ple_gate):
    depth = w_in.shape[0]
    h = x
    for i in range(depth):
        h = _layer(h, p[i], w_in[i], w_pool[i], pool_scale[i:i + 1], w_out[i], rel_bias,
                   g_mix_pre[i:i + 1], g_mix_post[i:i + 1], g_mlp_pre[i:i + 1],
                   g_mlp_post[i:i + 1], w_up[i], w_down[i], w_ple_proj[i], w_ple_gate[i])
    return h
```

```python
import functools
import math

import numpy as np
import jax
import jax.numpy as jnp
from jax import lax
from jax.experimental import pallas as pl
from jax.experimental.pallas import tpu as pltpu

D_MODEL = 1024
POOL_WIDTH = 512
POOL_WINDOWS = (2, 4, 8, 16)
POOL_CH = 128
POOL_HALO = 16
ATTN_WIDTH = 512
HEAD_DIM = 64
HEADS = 8
BLOCK = 256
TOP_K = 3
NUM_BUCKETS = 32
MAX_DISTANCE = 1024
D_FF = 4096
PLE_DIM = 256
EPS = 1e-6
NEG = -1e30
LOG2E = math.log2(math.e)
KAUG = 128
NEAR = 5
CHUNK = 4
CHUNK_KEYS = CHUNK * BLOCK
ACC_ROWS = HEAD_DIM + 16
SELECT_UNROLL = 8
LAG_OWN = 8
LAG_PREV = 7
LAG_FAR = 12
LAG_MAX = max(LAG_OWN, LAG_PREV, LAG_FAR)

TM_IN = 1024
TM_POST = 512
VMEM_LIMIT = 56 * 1024 * 1024


def _bucket_thresholds():
    n = np.arange(0, NEAR * BLOCK + BLOCK, dtype=np.int64)
    max_exact = NUM_BUCKETS // 2
    nf = np.maximum(n, 1).astype(np.float64)
    large = max_exact + (np.log(nf / max_exact) / math.log(MAX_DISTANCE / max_exact)
                         * (NUM_BUCKETS - max_exact)).astype(np.int64)
    large = np.minimum(large, NUM_BUCKETS - 1)
    bucket = np.where(n < max_exact, n, large)
    assert np.all(np.diff(bucket) >= 0)
    return [int(np.argmax(bucket >= b)) for b in range(NUM_BUCKETS)], bucket


_THRESH, _BUCKET = _bucket_thresholds()
assert _BUCKET[(NEAR - 1) * BLOCK + 1] == NUM_BUCKETS - 1
assert NEAR <= CHUNK + 1
assert POOL_WIDTH == ATTN_WIDTH


def _rms(x, g):
    ms = jnp.mean(x * x, axis=-1, keepdims=True)
    return x * lax.rsqrt(ms + EPS) * g


def _inproj_kernel(n_cast, x_ref, g_ref, wp_ref, wq_ref, wk_ref, wv_ref, wpool_ref, pscale_ref,
                   *refs):
    cast_in, refs = refs[:n_cast], refs[n_cast:]
    ypool_ref, qT_ref, kaug_ref, kbar_ref, vT_ref = refs[:5]
    cast_out, (halo_scr,) = refs[5:5 + n_cast], refs[5 + n_cast:]
    t = pl.program_id(1)

    for src, dst in zip(cast_in, cast_out):
        dst[...] = src[...].astype(dst.dtype)
    tm = x_ref.shape[1]
    nblk = tm // BLOCK

    @pl.when(t == 0)
    def _():
        halo_scr[...] = jnp.zeros(halo_scr.shape, jnp.float32)

    a = _rms(x_ref[0], g_ref[...]).astype(jnp.bfloat16)

    zk = jnp.dot(a, wk_ref[...], preferred_element_type=jnp.float32)
    lane = lax.broadcasted_iota(jnp.int32, (tm, KAUG), 1)
    gblk = t * nblk + lax.broadcasted_iota(jnp.int32, (tm, KAUG), 0) // BLOCK
    onehot = jnp.where((lane == HEAD_DIM + gblk) | (lane == HEAD_DIM + 32 + gblk), 1.0, 0.0)
    for h in range(HEADS):
        pair = zk[:, (h // 2) * KAUG:(h // 2 + 1) * KAUG]
        if h % 2:
            pair = pltpu.roll(pair, HEAD_DIM, axis=1)
        kaug_ref[0, h] = jnp.where(lane < HEAD_DIM, pair, onehot).astype(kaug_ref.dtype)
    for bl in range(nblk):
        kbar_ref[0, bl] = jnp.mean(zk[bl * BLOCK:(bl + 1) * BLOCK, :], axis=0, keepdims=True)

    u = jnp.dot(a, wp_ref[...], preferred_element_type=jnp.float32)

    zq = jnp.dot(a, wq_ref[...], preferred_element_type=jnp.float32)
    for bl in range(nblk):
        qT_ref[0, bl] = zq[bl * BLOCK:(bl + 1) * BLOCK, :].T

    tpos = t * tm + lax.broadcasted_iota(jnp.int32, (tm, 1), 0)
    for g, w in enumerate(POOL_WINDOWS):
        cs = slice(g * POOL_CH, (g + 1) * POOL_CH)
        ug = u[:, cs]
        acc = jnp.concatenate([halo_scr[:, cs], ug], axis=0)
        span = 1
        while span < w:
            acc = acc + pltpu.roll(acc, span, axis=0)
            span *= 2
        cnt = jnp.minimum(tpos + 1, w).astype(jnp.float32)
        d = acc[POOL_HALO:] / cnt - ug
        y = jnp.dot(d.astype(jnp.bfloat16), wpool_ref[g], preferred_element_type=jnp.float32)
        ypool_ref[0, :, cs] = (y * pscale_ref[:, cs]).astype(ypool_ref.dtype)
    halo_scr[...] = u[tm - POOL_HALO:tm, :]

    zv = jnp.dot(a, wv_ref[...], preferred_element_type=jnp.float32)
    for bl in range(nblk):
        vT_ref[0, bl] = zv[bl * BLOCK:(bl + 1) * BLOCK, :].T.astype(vT_ref.dtype)


def _bias_tile(tab_ref, h, d):
    r = lax.broadcasted_iota(jnp.int32, (BLOCK, BLOCK), 0)
    c = lax.broadcasted_iota(jnp.int32, (BLOCK, BLOCK), 1)
    n = d * BLOCK + c - r
    lo, hi = (0 if d == 0 else (d - 1) * BLOCK + 1), d * BLOCK + BLOCK - 1
    b0 = int(_BUCKET[lo])
    tile = jnp.full((BLOCK, BLOCK), tab_ref[b0, h] * LOG2E, jnp.float32)
    for b in range(b0 + 1, NUM_BUCKETS):
        if _THRESH[b] > hi:
            break
        tile = jnp.where(n >= _THRESH[b], tab_ref[b, h] * LOG2E, tile)
    if d == 0:
        tile = jnp.where(n >= 0, tile, NEG)
    return tile


def _fold8(x):
    return x.reshape(x.shape[0] // 8, 8, x.shape[1])


def _attn_items(nb):
    own = [(i, i // CHUNK) for i in range(nb)]
    prev = [(i, i // CHUNK - 1) for i in range(CHUNK, nb)]
    far = [(i, c) for i in range(nb) for c in range(i // CHUNK - 1)]
    return tuple(np.asarray(v, np.int32).T.copy() for v in (own, prev, far))


def _attn_kernel(tab_ref, own_ref, prev_ref, far_ref, qT_ref, kaug_ref, vT_ref, kbar_ref, oT_ref,
                 qaug_scr, s_buf, p_buf, m_st, acc_st, bias_scr):
    h = pl.program_id(0)
    b = pl.program_id(1)
    nb = kbar_ref.shape[1]

    @pl.when(b == 0)
    def _():
        for d in range(bias_scr.shape[0]):
            if d < NEAR:
                bias_scr[d] = _bias_tile(tab_ref, h, d)
            else:
                bias_scr[d] = jnp.zeros((BLOCK, BLOCK), jnp.float32)

    kb2 = kbar_ref[0, :, 0, :]
    kb = jnp.where(h % 2 == 0, kb2[:, :HEAD_DIM], kb2[:, HEAD_DIM:])
    kb_hi = kb.astype(jnp.bfloat16)
    kb_lo = (kb - kb_hi.astype(jnp.float32)).astype(jnp.bfloat16)
    jidx = lax.broadcasted_iota(jnp.int32, (nb, BLOCK), 0)
    cfar = jnp.full((nb, BLOCK), tab_ref[NUM_BUCKETS - 1, h] * LOG2E, jnp.float32)
    cfar_hi = cfar.astype(jnp.bfloat16).astype(jnp.float32)
    cfar_lo = cfar - cfar_hi

    def select(i):
        qT = qT_ref[0, i]
        q_hi = qT.astype(jnp.bfloat16)
        q_lo = (qT - q_hi.astype(jnp.float32)).astype(jnp.bfloat16)
        sc = (jnp.dot(kb_hi, q_hi, preferred_element_type=jnp.float32)
              + (jnp.dot(kb_hi, q_lo, preferred_element_type=jnp.float32)
                 + jnp.dot(kb_lo, q_hi, preferred_element_type=jnp.float32)))
        past = jidx < i
        work = jnp.where(past, sc, NEG)
        chosen = jnp.zeros((nb, BLOCK), jnp.bool_)
        for _ in range(TOP_K):
            mx = jnp.max(work, axis=0, keepdims=True)
            first = jnp.min(jnp.where(work == mx, jidx, nb), axis=0, keepdims=True)
            pick = jidx == first
            chosen = chosen | pick
            work = jnp.where(pick, -jnp.inf, work)
        sel = chosen & past
        far = jidx <= i - NEAR
        code_hi = jnp.where(past, jnp.where(sel, jnp.where(far, cfar_hi, 0.0), NEG),
                            jnp.where(jidx == i, 0.0, NEG))
        code_lo = jnp.where(sel & far, cfar_lo, 0.0)
        qaug_scr[i] = jnp.concatenate(
            [(qT * (HEAD_DIM ** -0.5 * LOG2E)).astype(jnp.bfloat16),
             code_hi.astype(jnp.bfloat16), code_lo.astype(jnp.bfloat16)], axis=0)

    def select_group(g, carry):
        for u in range(SELECT_UNROLL):
            select(g * SELECT_UNROLL + u)
        return carry

    lax.fori_loop(0, nb // SELECT_UNROLL, select_group, 0)

    def stage_qk(item, slot, near, nblk):
        i, c = item
        q_aug = qaug_scr[i]
        cm = None
        for w in range(nblk):
            row0 = pl.multiple_of((c * CHUNK + w) * BLOCK, BLOCK)
            sw = jnp.dot(kaug_ref[0, 0, pl.ds(row0, BLOCK), :], q_aug,
                         preferred_element_type=jnp.float32)
            if near:
                sw = sw + bias_scr[i - (c * CHUNK + w)]
            s_buf[slot, w] = sw
            fm = jnp.max(_fold8(sw), axis=0)
            cm = fm if cm is None else jnp.maximum(cm, fm)
        return cm

    def stage_exp(item, slot, cm, nblk, first):
        i, _ = item
        col_max = jnp.max(cm, axis=0, keepdims=True)
        if first:
            m_new = jnp.broadcast_to(col_max, cm.shape)
            alpha = None
        else:
            m_old = m_st[i]
            m_new = jnp.maximum(m_old, col_max)
            alpha = jnp.exp2(m_old - m_new)[0:1]
        m_st[i] = m_new
        m_row = m_new[0:1]
        for w in range(nblk):
            p = jnp.exp2(s_buf[slot, w] - m_row)
            p_buf[slot, w * BLOCK:(w + 1) * BLOCK, :] = p.astype(jnp.bfloat16)
        return alpha

    def ones_rows(keys):
        first = lax.broadcasted_iota(jnp.int32, (ACC_ROWS - HEAD_DIM, keys), 0) == 0
        return first.astype(jnp.bfloat16)

    def stage_pv(item, slot, alpha, nblk):
        i, c = item
        vblk = jnp.concatenate([vT_ref[0, c * CHUNK + w] for w in range(nblk)], axis=1)
        vblk = jnp.concatenate([vblk, ones_rows(nblk * BLOCK)], axis=0)
        pv = jnp.dot(vblk, p_buf[slot, :nblk * BLOCK, :],
                     preferred_element_type=jnp.float32)
        acc_st[i] = pv if alpha is None else alpha * acc_st[i] + pv

    def run(items_ref, near, lag, nblk_of_slot=lambda j: CHUNK, first=False):
        n = items_ref.shape[1]
        assert n % lag == 0 and n >= 2 * lag and lag <= s_buf.shape[0]
        item = lambda x: (items_ref[0, x], items_ref[1, x])

        def softmax_pv(x, j, cm):
            alpha = stage_exp(item(x), j, cm, nblk_of_slot(j), first)
            stage_pv(item(x), j, alpha, nblk_of_slot(j))

        cms = tuple(stage_qk(item(x), x, near, nblk_of_slot(x)) for x in range(lag))

        def body(k, cms):
            t = lag * k
            new = []
            for j in range(lag):
                softmax_pv(t - lag + j, j, cms[j])
                new.append(stage_qk(item(t + j), j, near, nblk_of_slot(j)))
            return tuple(new)

        cms = lax.fori_loop(1, n // lag, body, cms)
        for j in range(lag):
            softmax_pv(n - lag + j, j, cms[j])

    run(own_ref, True, LAG_OWN, lambda j: j % CHUNK + 1, first=True)
    run(prev_ref, True, LAG_PREV)
    run(far_ref, False, LAG_FAR)

    def finish(i, carry):
        acc = acc_st[i]
        oT_ref[0, i] = (acc[:HEAD_DIM] / acc[HEAD_DIM:HEAD_DIM + 1]).astype(oT_ref.dtype)
        return carry

    lax.fori_loop(0, nb, finish, 0)


def _post_kernel(x_ref, p_ref, ypool_ref, oT_ref, wop_ref, woa_ref, gpost_ref, gpre_ref,
                 gmpost_ref, wup_ref, wdown_ref, wgate_ref, wproj_ref, out_ref):
    nblk = oT_ref.shape[1]
    yat = jnp.concatenate([oT_ref[0, bl].T for bl in range(nblk)], axis=0)
    mix = (jnp.dot(ypool_ref[0], wop_ref[...], preferred_element_type=jnp.float32)
           + jnp.dot(yat.astype(jnp.bfloat16), woa_ref[...], preferred_element_type=jnp.float32))
    h1 = x_ref[0] + _rms(mix, gpost_ref[...])
    m = _rms(h1, gpre_ref[...]).astype(jnp.bfloat16)
    up = jnp.dot(m, wup_ref[...], preferred_element_type=jnp.float32)
    act = jnp.square(jnp.maximum(up, 0.0)).astype(jnp.bfloat16)
    f = jnp.dot(act, wdown_ref[...], preferred_element_type=jnp.float32)
    h2 = h1 + _rms(f, gmpost_ref[...])
    gate = jax.nn.sigmoid(jnp.dot(h2.astype(jnp.bfloat16), wgate_ref[...],
                                  preferred_element_type=jnp.float32))
    pe = jnp.dot(p_ref[0].astype(jnp.bfloat16), wproj_ref[...], preferred_element_type=jnp.float32)
    out_ref[0] = h2 + gate * pe


def _resident(shape):
    return pl.BlockSpec(shape, lambda *_: (0,) * len(shape), pipeline_mode=pl.Buffered(1))


def _mixers(x, w_in, w_pool, pool_scale, rel_bias, g_mix_pre, later_weights=()):
    B, S, _ = x.shape
    nb = S // BLOCK
    bf16 = jnp.bfloat16
    f32 = jnp.float32

    wp = w_in[:, :POOL_WIDTH].astype(bf16)
    wq = w_in[:, POOL_WIDTH:POOL_WIDTH + ATTN_WIDTH].astype(bf16)
    wk = w_in[:, POOL_WIDTH + ATTN_WIDTH:POOL_WIDTH + 2 * ATTN_WIDTH].astype(bf16)
    wv = w_in[:, POOL_WIDTH + 2 * ATTN_WIDTH:].astype(bf16)

    tm = TM_IN
    nblk = tm // BLOCK
    steps_b = S // tm
    slab = lambda w: pl.BlockSpec((w.shape[0] // (B * steps_b), w.shape[1]),
                                  lambda b, t: (b * steps_b + t, 0))
    ypool, qT, kaug, kbar, vT, *cast = pl.pallas_call(
        functools.partial(_inproj_kernel, len(later_weights)),
        grid=(B, steps_b),
        in_specs=[
            pl.BlockSpec((1, tm, D_MODEL), lambda b, t: (b, t, 0)),
            _resident((1, D_MODEL)),
            _resident((D_MODEL, POOL_WIDTH)),
            _resident((D_MODEL, ATTN_WIDTH)),
            _resident((D_MODEL, ATTN_WIDTH)),
            _resident((D_MODEL, ATTN_WIDTH)),
            _resident((len(POOL_WINDOWS), POOL_CH, POOL_CH)),
            _resident((1, POOL_WIDTH)),
            *[slab(w) for w in later_weights],
        ],
        out_specs=[
            pl.BlockSpec((1, tm, POOL_WIDTH), lambda b, t: (b, t, 0)),
            pl.BlockSpec((1, nblk, ATTN_WIDTH, BLOCK), lambda b, t: (b, t, 0, 0)),
            pl.BlockSpec((1, HEADS, tm, KAUG), lambda b, t: (b, 0, t, 0)),
            pl.BlockSpec((1, nblk, 1, ATTN_WIDTH), lambda b, t: (b, t, 0, 0)),
            pl.BlockSpec((1, nblk, ATTN_WIDTH, BLOCK), lambda b, t: (b, t, 0, 0)),
            *[slab(w) for w in later_weights],
        ],
        out_shape=[
            jax.ShapeDtypeStruct((B, S, POOL_WIDTH), bf16),
            jax.ShapeDtypeStruct((B, nb, ATTN_WIDTH, BLOCK), f32),
            jax.ShapeDtypeStruct((B, HEADS, S, KAUG), bf16),
            jax.ShapeDtypeStruct((B, nb, 1, ATTN_WIDTH), f32),
            jax.ShapeDtypeStruct((B, nb, ATTN_WIDTH, BLOCK), bf16),
            *[jax.ShapeDtypeStruct(w.shape, bf16) for w in later_weights],
        ],
        scratch_shapes=[pltpu.VMEM((POOL_HALO, POOL_WIDTH), f32)],
        compiler_params=pltpu.CompilerParams(
            dimension_semantics=("arbitrary", "arbitrary"), vmem_limit_bytes=VMEM_LIMIT),
        name="inproj_pool",
    )(x, g_mix_pre, wp, wq, wk, wv, w_pool.astype(bf16), pool_scale, *later_weights)

    own_items, prev_items, far_items = _attn_items(nb)
    smem = pl.BlockSpec(memory_space=pltpu.SMEM)
    oT = pl.pallas_call(
        _attn_kernel,
        grid=(HEADS, B),
        in_specs=[
            smem, smem, smem, smem,
            pl.BlockSpec((1, nb, HEAD_DIM, BLOCK), lambda h, b: (b, 0, h, 0)),
            pl.BlockSpec((1, 1, S, KAUG), lambda h, b: (b, h, 0, 0)),
            pl.BlockSpec((1, nb, HEAD_DIM, BLOCK), lambda h, b: (b, 0, h, 0)),
            pl.BlockSpec((1, nb, 1, 2 * HEAD_DIM), lambda h, b: (b, 0, 0, h // 2)),
        ],
        out_specs=pl.BlockSpec((1, nb, HEAD_DIM, BLOCK), lambda h, b: (b, 0, h, 0)),
        out_shape=jax.ShapeDtypeStruct((B, nb, ATTN_WIDTH, BLOCK), f32),
        scratch_shapes=[
            pltpu.VMEM((nb, KAUG, BLOCK), bf16),
            pltpu.VMEM((LAG_MAX, CHUNK, BLOCK, BLOCK), f32),
            pltpu.VMEM((LAG_MAX, CHUNK_KEYS, BLOCK), bf16),
            pltpu.VMEM((nb, 8, BLOCK), f32),
            pltpu.VMEM((nb, ACC_ROWS, BLOCK), f32),
            pltpu.VMEM((2 * CHUNK, BLOCK, BLOCK), f32),
        ],
        compiler_params=pltpu.CompilerParams(
            dimension_semantics=("arbitrary", "arbitrary"), vmem_limit_bytes=VMEM_LIMIT),
        name="moba_attention",
    )(rel_bias, jnp.asarray(own_items), jnp.asarray(prev_items), jnp.asarray(far_items),
      qT, kaug, vT, kbar)
    return ypool, oT, cast


def _layer(x, p, w_in, w_pool, pool_scale, w_out, rel_bias, g_mix_pre, g_mix_post,
           g_mlp_pre, g_mlp_post, w_up, w_down, w_ple_proj, w_ple_gate):
    B, S, _ = x.shape
    bf16 = jnp.bfloat16
    f32 = jnp.float32
    ypool, oT, (wo, wup, wdown, wgate, wproj) = _mixers(
        x, w_in, w_pool, pool_scale, rel_bias, g_mix_pre,
        later_weights=(w_out, w_up, w_down, w_ple_gate, w_ple_proj))

    tm = TM_POST
    nblk = tm // BLOCK
    out = pl.pallas_call(
        _post_kernel,
        grid=(B, S // tm),
        in_specs=[
            pl.BlockSpec((1, tm, D_MODEL), lambda b, t: (b, t, 0)),
            pl.BlockSpec((1, tm, PLE_DIM), lambda b, t: (b, t, 0)),
            pl.BlockSpec((1, tm, POOL_WIDTH), lambda b, t: (b, t, 0)),
            pl.BlockSpec((1, nblk, ATTN_WIDTH, BLOCK), lambda b, t: (b, t, 0, 0)),
            pl.BlockSpec((POOL_WIDTH, D_MODEL), lambda b, t: (0, 0), pipeline_mode=pl.Buffered(1)),
            pl.BlockSpec((ATTN_WIDTH, D_MODEL), lambda b, t: (1, 0), pipeline_mode=pl.Buffered(1)),
            _resident((1, D_MODEL)),
            _resident((1, D_MODEL)),
            _resident((1, D_MODEL)),
            _resident((D_MODEL, D_FF)),
            _resident((D_FF, D_MODEL)),
            _resident((D_MODEL, D_MODEL)),
            _resident((PLE_DIM, D_MODEL)),
        ],
        out_specs=pl.BlockSpec((1, tm, D_MODEL), lambda b, t: (b, t, 0)),
        out_shape=jax.ShapeDtypeStruct((B, S, D_MODEL), f32),
        compiler_params=pltpu.CompilerParams(
            dimension_semantics=("arbitrary", "arbitrary"), vmem_limit_bytes=VMEM_LIMIT),
        name="post_mlp",
    )(x, p, ypool, oT, wo, wo, g_mix_post, g_mlp_pre, g_mlp_post, wup, wdown, wgate, wproj)
    return out


def kernel(x, p, w_in, w_pool, pool_scale, w_out, rel_bias, g_mix_pre, g_mix_post,
           g_mlp_pre, g_mlp_post, w_up, w_down, w_ple_proj, w_ple_gate):
    depth = w_in.shape[0]
    h = x
    for i in range(depth):
        h = _layer(h, p[i], w_in[i], w_pool[i], pool_scale[i:i + 1], w_out[i], rel_bias,
                   g_mix_pre[i:i + 1], g_mix_post[i:i + 1], g_mlp_pre[i:i + 1],
                   g_mlp_post[i:i + 1], w_up[i], w_down[i], w_ple_proj[i], w_ple_gate[i])
    return h
```

```python
import functools
import math

import numpy as np
import jax
import jax.numpy as jnp
from jax import lax
from jax.experimental import pallas as pl
from jax.experimental.pallas import tpu as pltpu

D_MODEL = 1024
POOL_WIDTH = 512
POOL_WINDOWS = (2, 4, 8, 16)
POOL_CH = 128
POOL_HALO = 16
ATTN_WIDTH = 512
HEAD_DIM = 64
HEADS = 8
BLOCK = 256
TOP_K = 3
NUM_BUCKETS = 32
MAX_DISTANCE = 1024
D_FF = 4096
PLE_DIM = 256
EPS = 1e-6
NEG = -1e30
LOG2E = math.log2(math.e)
KAUG = 128
NEAR = 5
CHUNK = 4
CHUNK_KEYS = CHUNK * BLOCK
ACC_ROWS = HEAD_DIM + 16
SELECT_UNROLL = 8
LAG_OWN = 8
LAG_PREV = 7
LAG_FAR = 12
LAG_MAX = max(LAG_OWN, LAG_PREV, LAG_FAR)

TM_IN = 1024
TM_POST = 512
VMEM_LIMIT = 56 * 1024 * 1024


def _bucket_thresholds():
    n = np.arange(0, NEAR * BLOCK + BLOCK, dtype=np.int64)
    max_exact = NUM_BUCKETS // 2
    nf = np.maximum(n, 1).astype(np.float64)
    large = max_exact + (np.log(nf / max_exact) / math.log(MAX_DISTANCE / max_exact)
                         * (NUM_BUCKETS - max_exact)).astype(np.int64)
    large = np.minimum(large, NUM_BUCKETS - 1)
    bucket = np.where(n < max_exact, n, large)
    assert np.all(np.diff(bucket) >= 0)
    return [int(np.argmax(bucket >= b)) for b in range(NUM_BUCKETS)], bucket


_THRESH, _BUCKET = _bucket_thresholds()
assert _BUCKET[(NEAR - 1) * BLOCK + 1] == NUM_BUCKETS - 1
assert NEAR <= CHUNK + 1
assert POOL_WIDTH == ATTN_WIDTH


def _rms(x, g):
    ms = jnp.mean(x * x, axis=-1, keepdims=True)
    return x * lax.rsqrt(ms + EPS) * g


def _inproj_kernel(n_cast, x_ref, g_ref, win_ref, wpool_ref, pscale_ref, *refs):
    cast_in, refs = refs[:n_cast], refs[n_cast:]
    ypool_ref, qT_ref, kaug_ref, kbar_ref, vT_ref = refs[:5]
    cast_out, (halo_scr, win_scr, wpool_scr) = refs[5:5 + n_cast], refs[5 + n_cast:]
    t = pl.program_id(1)

    @pl.when((pl.program_id(0) == 0) & (t == 0))
    def _():
        win_scr[...] = win_ref[...].astype(win_scr.dtype)
        wpool_scr[...] = wpool_ref[...].astype(wpool_scr.dtype)

    def w_cols(start, width):
        return win_scr[:, start:start + width]

    for src, dst in zip(cast_in, cast_out):
        dst[...] = src[...].astype(dst.dtype)
    tm = x_ref.shape[1]
    nblk = tm // BLOCK

    @pl.when(t == 0)
    def _():
        halo_scr[...] = jnp.zeros(halo_scr.shape, jnp.float32)

    a = _rms(x_ref[0], g_ref[...]).astype(jnp.bfloat16)

    zk = jnp.dot(a, w_cols(POOL_WIDTH + ATTN_WIDTH, ATTN_WIDTH),
                 preferred_element_type=jnp.float32)
    lane = lax.broadcasted_iota(jnp.int32, (tm, KAUG), 1)
    gblk = t * nblk + lax.broadcasted_iota(jnp.int32, (tm, KAUG), 0) // BLOCK
    onehot = jnp.where((lane == HEAD_DIM + gblk) | (lane == HEAD_DIM + 32 + gblk), 1.0, 0.0)
    for h in range(HEADS):
        pair = zk[:, (h // 2) * KAUG:(h // 2 + 1) * KAUG]
        if h % 2:
            pair = pltpu.roll(pair, HEAD_DIM, axis=1)
        kaug_ref[0, h] = jnp.where(lane < HEAD_DIM, pair, onehot).astype(kaug_ref.dtype)
    for bl in range(nblk):
        kbar_ref[0, bl] = jnp.mean(zk[bl * BLOCK:(bl + 1) * BLOCK, :], axis=0, keepdims=True)

    u = jnp.dot(a, w_cols(0, POOL_WIDTH), preferred_element_type=jnp.float32)

    zq = jnp.dot(a, w_cols(POOL_WIDTH, ATTN_WIDTH), preferred_element_type=jnp.float32)
    for bl in range(nblk):
        qT_ref[0, bl] = zq[bl * BLOCK:(bl + 1) * BLOCK, :].T

    tpos = t * tm + lax.broadcasted_iota(jnp.int32, (tm, 1), 0)
    for g, w in enumerate(POOL_WINDOWS):
        cs = slice(g * POOL_CH, (g + 1) * POOL_CH)
        ug = u[:, cs]
        acc = jnp.concatenate([halo_scr[:, cs], ug], axis=0)
        span = 1
        while span < w:
            acc = acc + pltpu.roll(acc, span, axis=0)
            span *= 2
        cnt = jnp.minimum(tpos + 1, w).astype(jnp.float32)
        d = acc[POOL_HALO:] / cnt - ug
        y = jnp.dot(d.astype(jnp.bfloat16), wpool_scr[g], preferred_element_type=jnp.float32)
        ypool_ref[0, :, cs] = (y * pscale_ref[:, cs]).astype(ypool_ref.dtype)
    halo_scr[...] = u[tm - POOL_HALO:tm, :]

    zv = jnp.dot(a, w_cols(POOL_WIDTH + 2 * ATTN_WIDTH, ATTN_WIDTH),
                 preferred_element_type=jnp.float32)
    for bl in range(nblk):
        vT_ref[0, bl] = zv[bl * BLOCK:(bl + 1) * BLOCK, :].T.astype(vT_ref.dtype)


def _bias_tile(tab_ref, h, d):
    r = lax.broadcasted_iota(jnp.int32, (BLOCK, BLOCK), 0)
    c = lax.broadcasted_iota(jnp.int32, (BLOCK, BLOCK), 1)
    n = d * BLOCK + c - r
    lo, hi = (0 if d == 0 else (d - 1) * BLOCK + 1), d * BLOCK + BLOCK - 1
    b0 = int(_BUCKET[lo])
    tile = jnp.full((BLOCK, BLOCK), tab_ref[b0, h] * LOG2E, jnp.float32)
    for b in range(b0 + 1, NUM_BUCKETS):
        if _THRESH[b] > hi:
            break
        tile = jnp.where(n >= _THRESH[b], tab_ref[b, h] * LOG2E, tile)
    if d == 0:
        tile = jnp.where(n >= 0, tile, NEG)
    return tile


def _fold8(x):
    return x.reshape(x.shape[0] // 8, 8, x.shape[1])


def _attn_items(nb):
    own = [(i, i // CHUNK) for i in range(nb)]
    prev = [(i, i // CHUNK - 1) for i in range(CHUNK, nb)]
    far = [(i, c) for i in range(nb) for c in range(i // CHUNK - 1)]
    return tuple(np.asarray(v, np.int32).T.copy() for v in (own, prev, far))


def _attn_kernel(tab_ref, own_ref, prev_ref, far_ref, qT_ref, kaug_ref, vT_ref, kbar_ref, oT_ref,
                 qaug_scr, s_buf, p_buf, m_st, acc_st, bias_scr):
    h = pl.program_id(0)
    b = pl.program_id(1)
    nb = kbar_ref.shape[1]

    @pl.when(b == 0)
    def _():
        for d in range(bias_scr.shape[0]):
            if d < NEAR:
                bias_scr[d] = _bias_tile(tab_ref, h, d)
            else:
                bias_scr[d] = jnp.zeros((BLOCK, BLOCK), jnp.float32)

    kb2 = kbar_ref[0, :, 0, :]
    kb = jnp.where(h % 2 == 0, kb2[:, :HEAD_DIM], kb2[:, HEAD_DIM:])
    kb_hi = kb.astype(jnp.bfloat16)
    kb_lo = (kb - kb_hi.astype(jnp.float32)).astype(jnp.bfloat16)
    jidx = lax.broadcasted_iota(jnp.int32, (nb, BLOCK), 0)
    cfar = jnp.full((nb, BLOCK), tab_ref[NUM_BUCKETS - 1, h] * LOG2E, jnp.float32)
    cfar_hi = cfar.astype(jnp.bfloat16).astype(jnp.float32)
    cfar_lo = cfar - cfar_hi

    def select(i):
        qT = qT_ref[0, i]
        q_hi = qT.astype(jnp.bfloat16)
        q_lo = (qT - q_hi.astype(jnp.float32)).astype(jnp.bfloat16)
        sc = (jnp.dot(kb_hi, q_hi, preferred_element_type=jnp.float32)
              + (jnp.dot(kb_hi, q_lo, preferred_element_type=jnp.float32)
                 + jnp.dot(kb_lo, q_hi, preferred_element_type=jnp.float32)))
        past = jidx < i
        work = jnp.where(past, sc, NEG)
        chosen = jnp.zeros((nb, BLOCK), jnp.bool_)
        for _ in range(TOP_K):
            mx = jnp.max(work, axis=0, keepdims=True)
            first = jnp.min(jnp.where(work == mx, jidx, nb), axis=0, keepdims=True)
            pick = jidx == first
            chosen = chosen | pick
            work = jnp.where(pick, -jnp.inf, work)
        sel = chosen & past
        far = jidx <= i - NEAR
        code_hi = jnp.where(past, jnp.where(sel, jnp.where(far, cfar_hi, 0.0), NEG),
                            jnp.where(jidx == i, 0.0, NEG))
        code_lo = jnp.where(sel & far, cfar_lo, 0.0)
        qaug_scr[i] = jnp.concatenate(
            [(qT * (HEAD_DIM ** -0.5 * LOG2E)).astype(jnp.bfloat16),
             code_hi.astype(jnp.bfloat16), code_lo.astype(jnp.bfloat16)], axis=0)

    def select_group(g, carry):
        for u in range(SELECT_UNROLL):
            select(g * SELECT_UNROLL + u)
        return carry

    lax.fori_loop(0, nb // SELECT_UNROLL, select_group, 0)

    def stage_qk(item, slot, near, nblk):
        i, c = item
        q_aug = qaug_scr[i]
        cm = None
        for w in range(nblk):
            row0 = pl.multiple_of((c * CHUNK + w) * BLOCK, BLOCK)
            sw = jnp.dot(kaug_ref[0, 0, pl.ds(row0, BLOCK), :], q_aug,
                         preferred_element_type=jnp.float32)
            if near:
                sw = sw + bias_scr[i - (c * CHUNK + w)]
            s_buf[slot, w] = sw
            fm = jnp.max(_fold8(sw), axis=0)
            cm = fm if cm is None else jnp.maximum(cm, fm)
        return cm

    def stage_exp(item, slot, cm, nblk, first):
        i, _ = item
        col_max = jnp.max(cm, axis=0, keepdims=True)
        if first:
            m_new = jnp.broadcast_to(col_max, cm.shape)
            alpha = None
        else:
            m_old = m_st[i]
            m_new = jnp.maximum(m_old, col_max)
            alpha = jnp.exp2(m_old - m_new)[0:1]
        m_st[i] = m_new
        m_row = m_new[0:1]
        for w in range(nblk):
            p = jnp.exp2(s_buf[slot, w] - m_row)
            p_buf[slot, w * BLOCK:(w + 1) * BLOCK, :] = p.astype(jnp.bfloat16)
        return alpha

    def ones_rows(keys):
        first = lax.broadcasted_iota(jnp.int32, (ACC_ROWS - HEAD_DIM, keys), 0) == 0
        return first.astype(jnp.bfloat16)

    def stage_pv(item, slot, alpha, nblk):
        i, c = item
        vblk = jnp.concatenate([vT_ref[0, c * CHUNK + w] for w in range(nblk)], axis=1)
        vblk = jnp.concatenate([vblk, ones_rows(nblk * BLOCK)], axis=0)
        pv = jnp.dot(vblk, p_buf[slot, :nblk * BLOCK, :],
                     preferred_element_type=jnp.float32)
        acc_st[i] = pv if alpha is None else alpha * acc_st[i] + pv

    def run(items_ref, near, lag, nblk_of_slot=lambda j: CHUNK, first=False):
        n = items_ref.shape[1]
        assert n % lag == 0 and n >= 2 * lag and lag <= s_buf.shape[0]
        item = lambda x: (items_ref[0, x], items_ref[1, x])

        def softmax_pv(x, j, cm):
            alpha = stage_exp(item(x), j, cm, nblk_of_slot(j), first)
            stage_pv(item(x), j, alpha, nblk_of_slot(j))

        cms = tuple(stage_qk(item(x), x, near, nblk_of_slot(x)) for x in range(lag))

        def body(k, cms):
            t = lag * k
            new = []
            for j in range(lag):
                softmax_pv(t - lag + j, j, cms[j])
                new.append(stage_qk(item(t + j), j, near, nblk_of_slot(j)))
            return tuple(new)

        cms = lax.fori_loop(1, n // lag, body, cms)
        for j in range(lag):
            softmax_pv(n - lag + j, j, cms[j])

    run(own_ref, True, LAG_OWN, lambda j: j % CHUNK + 1, first=True)
    run(prev_ref, True, LAG_PREV)
    run(far_ref, False, LAG_FAR)

    def finish(i, carry):
        acc = acc_st[i]
        oT_ref[0, i] = (acc[:HEAD_DIM] / acc[HEAD_DIM:HEAD_DIM + 1]).astype(oT_ref.dtype)
        return carry

    lax.fori_loop(0, nb, finish, 0)


def _post_kernel(x_ref, p_ref, ypool_ref, oT_ref, wop_ref, woa_ref, gpost_ref, gpre_ref,
                 gmpost_ref, wup_ref, wdown_ref, wgate_ref, wproj_ref, out_ref):
    nblk = oT_ref.shape[1]
    yat = jnp.concatenate([oT_ref[0, bl].T for bl in range(nblk)], axis=0)
    mix = (jnp.dot(ypool_ref[0], wop_ref[...], preferred_element_type=jnp.float32)
           + jnp.dot(yat.astype(jnp.bfloat16), woa_ref[...], preferred_element_type=jnp.float32))
    h1 = x_ref[0] + _rms(mix, gpost_ref[...])
    m = _rms(h1, gpre_ref[...]).astype(jnp.bfloat16)
    up = jnp.dot(m, wup_ref[...], preferred_element_type=jnp.float32)
    act = jnp.square(jnp.maximum(up, 0.0)).astype(jnp.bfloat16)
    f = jnp.dot(act, wdown_ref[...], preferred_element_type=jnp.float32)
    h2 = h1 + _rms(f, gmpost_ref[...])
    gate = jax.nn.sigmoid(jnp.dot(h2.astype(jnp.bfloat16), wgate_ref[...],
                                  preferred_element_type=jnp.float32))
    pe = jnp.dot(p_ref[0].astype(jnp.bfloat16), wproj_ref[...], preferred_element_type=jnp.float32)
    out_ref[0] = h2 + gate * pe


def _resident(shape):
    return pl.BlockSpec(shape, lambda *_: (0,) * len(shape), pipeline_mode=pl.Buffered(1))


def _mixers(x, w_in, w_pool, pool_scale, rel_bias, g_mix_pre, later_weights=()):
    B, S, _ = x.shape
    nb = S // BLOCK
    bf16 = jnp.bfloat16
    f32 = jnp.float32

    tm = TM_IN
    nblk = tm // BLOCK
    steps_b = S // tm
    slab = lambda w: pl.BlockSpec((w.shape[0] // (B * steps_b), w.shape[1]),
                                  lambda b, t: (b * steps_b + t, 0))
    ypool, qT, kaug, kbar, vT, *cast = pl.pallas_call(
        functools.partial(_inproj_kernel, len(later_weights)),
        grid=(B, steps_b),
        in_specs=[
            pl.BlockSpec((1, tm, D_MODEL), lambda b, t: (b, t, 0)),
            _resident((1, D_MODEL)),
            _resident(w_in.shape),
            _resident(w_pool.shape),
            _resident((1, POOL_WIDTH)),
            *[slab(w) for w in later_weights],
        ],
        out_specs=[
            pl.BlockSpec((1, tm, POOL_WIDTH), lambda b, t: (b, t, 0)),
            pl.BlockSpec((1, nblk, ATTN_WIDTH, BLOCK), lambda b, t: (b, t, 0, 0)),
            pl.BlockSpec((1, HEADS, tm, KAUG), lambda b, t: (b, 0, t, 0)),
            pl.BlockSpec((1, nblk, 1, ATTN_WIDTH), lambda b, t: (b, t, 0, 0)),
            pl.BlockSpec((1, nblk, ATTN_WIDTH, BLOCK), lambda b, t: (b, t, 0, 0)),
            *[slab(w) for w in later_weights],
        ],
        out_shape=[
            jax.ShapeDtypeStruct((B, S, POOL_WIDTH), bf16),
            jax.ShapeDtypeStruct((B, nb, ATTN_WIDTH, BLOCK), f32),
            jax.ShapeDtypeStruct((B, HEADS, S, KAUG), bf16),
            jax.ShapeDtypeStruct((B, nb, 1, ATTN_WIDTH), f32),
            jax.ShapeDtypeStruct((B, nb, ATTN_WIDTH, BLOCK), bf16),
            *[jax.ShapeDtypeStruct(w.shape, bf16) for w in later_weights],
        ],
        scratch_shapes=[pltpu.VMEM((POOL_HALO, POOL_WIDTH), f32),
                        pltpu.VMEM(w_in.shape, bf16), pltpu.VMEM(w_pool.shape, bf16)],
        compiler_params=pltpu.CompilerParams(
            dimension_semantics=("arbitrary", "arbitrary"), vmem_limit_bytes=VMEM_LIMIT),
        name="inproj_pool",
    )(x, g_mix_pre, w_in, w_pool, pool_scale, *later_weights)

    own_items, prev_items, far_items = _attn_items(nb)
    smem = pl.BlockSpec(memory_space=pltpu.SMEM)
    oT = pl.pallas_call(
        _attn_kernel,
        grid=(HEADS, B),
        in_specs=[
            smem, smem, smem, smem,
            pl.BlockSpec((1, nb, HEAD_DIM, BLOCK), lambda h, b: (b, 0, h, 0)),
            pl.BlockSpec((1, 1, S, KAUG), lambda h, b: (b, h, 0, 0)),
            pl.BlockSpec((1, nb, HEAD_DIM, BLOCK), lambda h, b: (b, 0, h, 0)),
            pl.BlockSpec((1, nb, 1, 2 * HEAD_DIM), lambda h, b: (b, 0, 0, h // 2)),
        ],
        out_specs=pl.BlockSpec((1, nb, HEAD_DIM, BLOCK), lambda h, b: (b, 0, h, 0)),
        out_shape=jax.ShapeDtypeStruct((B, nb, ATTN_WIDTH, BLOCK), f32),
        scratch_shapes=[
            pltpu.VMEM((nb, KAUG, BLOCK), bf16),
            pltpu.VMEM((LAG_MAX, CHUNK, BLOCK, BLOCK), f32),
            pltpu.VMEM((LAG_MAX, CHUNK_KEYS, BLOCK), bf16),
            pltpu.VMEM((nb, 8, BLOCK), f32),
            pltpu.VMEM((nb, ACC_ROWS, BLOCK), f32),
            pltpu.VMEM((2 * CHUNK, BLOCK, BLOCK), f32),
        ],
        compiler_params=pltpu.CompilerParams(
            dimension_semantics=("arbitrary", "arbitrary"), vmem_limit_bytes=VMEM_LIMIT),
        name="moba_attention",
    )(rel_bias, jnp.asarray(own_items), jnp.asarray(prev_items), jnp.asarray(far_items),
      qT, kaug, vT, kbar)
    return ypool, oT, cast


def _layer(x, p, w_in, w_pool, pool_scale, w_out, rel_bias, g_mix_pre, g_mix_post,
           g_mlp_pre, g_mlp_post, w_up, w_down, w_ple_proj, w_ple_gate):
    B, S, _ = x.shape
    bf16 = jnp.bfloat16
    f32 = jnp.float32
    ypool, oT, (wo, wup, wdown, wgate, wproj) = _mixers(
        x, w_in, w_pool, pool_scale, rel_bias, g_mix_pre,
        later_weights=(w_out, w_up, w_down, w_ple_gate, w_ple_proj))

    tm = TM_POST
    nblk = tm // BLOCK
    out = pl.pallas_call(
        _post_kernel,
        grid=(B, S // tm),
        in_specs=[
            pl.BlockSpec((1, tm, D_MODEL), lambda b, t: (b, t, 0)),
            pl.BlockSpec((1, tm, PLE_DIM), lambda b, t: (b, t, 0)),
            pl.BlockSpec((1, tm, POOL_WIDTH), lambda b, t: (b, t, 0)),
            pl.BlockSpec((1, nblk, ATTN_WIDTH, BLOCK), lambda b, t: (b, t, 0, 0)),
            pl.BlockSpec((POOL_WIDTH, D_MODEL), lambda b, t: (0, 0), pipeline_mode=pl.Buffered(1)),
            pl.BlockSpec((ATTN_WIDTH, D_MODEL), lambda b, t: (1, 0), pipeline_mode=pl.Buffered(1)),
            _resident((1, D_MODEL)),
            _resident((1, D_MODEL)),
            _resident((1, D_MODEL)),
            _resident((D_MODEL, D_FF)),
            _resident((D_FF, D_MODEL)),
            _resident((D_MODEL, D_MODEL)),
            _resident((PLE_DIM, D_MODEL)),
        ],
        out_specs=pl.BlockSpec((1, tm, D_MODEL), lambda b, t: (b, t, 0)),
        out_shape=jax.ShapeDtypeStruct((B, S, D_MODEL), f32),
        compiler_params=pltpu.CompilerParams(
            dimension_semantics=("arbitrary", "arbitrary"), vmem_limit_bytes=VMEM_LIMIT),
        name="post_mlp",
    )(x, p, ypool, oT, wo, wo, g_mix_post, g_mlp_pre, g_mlp_post, wup, wdown, wgate, wproj)
    return out


def kernel(x, p, w_in, w_pool, pool_scale, w_out, rel_bias, g_mix_pre, g_mix_post,
           g_mlp_pre, g_mlp_post, w_up, w_down, w_ple_proj, w_ple_gate):
    depth = w_in.shape[0]
    h = x
    for i in range(depth):
        h = _layer(h, p[i], w_in[i], w_pool[i], pool_scale[i:i + 1], w_out[i], rel_bias,
                   g_mix_pre[i:i + 1], g_mix_post[i:i + 1], g_mlp_pre[i:i + 1],
                   g_mlp_post[i:i + 1], w_up[i], w_down[i], w_ple_proj[i], w_ple_gate[i])
    return h
```

```python
import functools
import math

import numpy as np
import jax
import jax.numpy as jnp
from jax import lax
from jax.experimental import pallas as pl
from jax.experimental.pallas import tpu as pltpu

D_MODEL = 1024
POOL_WIDTH = 512
POOL_WINDOWS = (2, 4, 8, 16)
POOL_CH = 128
POOL_HALO = 16
ATTN_WIDTH = 512
HEAD_DIM = 64
HEADS = 8
BLOCK = 256
TOP_K = 3
NUM_BUCKETS = 32
MAX_DISTANCE = 1024
D_FF = 4096
PLE_DIM = 256
EPS = 1e-6
NEG = -1e30
LOG2E = math.log2(math.e)
KAUG = 128
NEAR = 5
CHUNK = 4
CHUNK_KEYS = CHUNK * BLOCK
ACC_ROWS = HEAD_DIM + 16
SELECT_UNROLL = 16
LAG_OWN = 8
LAG_PREV = 7
LAG_FAR = 12
LAG_MAX = max(LAG_OWN, LAG_PREV, LAG_FAR)

TM_IN = 1024
TM_POST = 512
VMEM_LIMIT = 56 * 1024 * 1024


def _bucket_thresholds():
    n = np.arange(0, NEAR * BLOCK + BLOCK, dtype=np.int64)
    max_exact = NUM_BUCKETS // 2
    nf = np.maximum(n, 1).astype(np.float64)
    large = max_exact + (np.log(nf / max_exact) / math.log(MAX_DISTANCE / max_exact)
                         * (NUM_BUCKETS - max_exact)).astype(np.int64)
    large = np.minimum(large, NUM_BUCKETS - 1)
    bucket = np.where(n < max_exact, n, large)
    assert np.all(np.diff(bucket) >= 0)
    return [int(np.argmax(bucket >= b)) for b in range(NUM_BUCKETS)], bucket


_THRESH, _BUCKET = _bucket_thresholds()
assert _BUCKET[(NEAR - 1) * BLOCK + 1] == NUM_BUCKETS - 1
assert NEAR <= CHUNK + 1
assert POOL_WIDTH == ATTN_WIDTH


def _rms(x, g):
    ms = jnp.mean(x * x, axis=-1, keepdims=True)
    return x * lax.rsqrt(ms + EPS) * g


def _inproj_kernel(n_cast, x_ref, g_ref, win_ref, wpool_ref, pscale_ref, *refs):
    cast_in, refs = refs[:n_cast], refs[n_cast:]
    ypool_ref, qT_ref, kaug_ref, kbar_ref, vT_ref = refs[:5]
    cast_out, (halo_scr, win_scr, wpool_scr) = refs[5:5 + n_cast], refs[5 + n_cast:]
    t = pl.program_id(1)

    @pl.when((pl.program_id(0) == 0) & (t == 0))
    def _():
        win_scr[...] = win_ref[...].astype(win_scr.dtype)
        wpool_scr[...] = wpool_ref[...].astype(wpool_scr.dtype)

    def w_cols(start, width):
        return win_scr[:, start:start + width]

    for src, dst in zip(cast_in, cast_out):
        dst[...] = src[...].astype(dst.dtype)
    tm = x_ref.shape[1]
    nblk = tm // BLOCK

    @pl.when(t == 0)
    def _():
        halo_scr[...] = jnp.zeros(halo_scr.shape, jnp.float32)

    a = _rms(x_ref[0], g_ref[...]).astype(jnp.bfloat16)

    zk = jnp.dot(a, w_cols(POOL_WIDTH + ATTN_WIDTH, ATTN_WIDTH),
                 preferred_element_type=jnp.float32)
    lane = lax.broadcasted_iota(jnp.int32, (tm, KAUG), 1)
    gblk = t * nblk + lax.broadcasted_iota(jnp.int32, (tm, KAUG), 0) // BLOCK
    onehot = jnp.where((lane == HEAD_DIM + gblk) | (lane == HEAD_DIM + 32 + gblk), 1.0, 0.0)
    for h in range(HEADS):
        pair = zk[:, (h // 2) * KAUG:(h // 2 + 1) * KAUG]
        if h % 2:
            pair = pltpu.roll(pair, HEAD_DIM, axis=1)
        kaug_ref[0, h] = jnp.where(lane < HEAD_DIM, pair, onehot).astype(kaug_ref.dtype)
    for bl in range(nblk):
        kbar_ref[0, bl] = jnp.mean(zk[bl * BLOCK:(bl + 1) * BLOCK, :], axis=0, keepdims=True)

    u = jnp.dot(a, w_cols(0, POOL_WIDTH), preferred_element_type=jnp.float32)

    zq = jnp.dot(a, w_cols(POOL_WIDTH, ATTN_WIDTH), preferred_element_type=jnp.float32)
    for bl in range(nblk):
        qT_ref[0, bl] = zq[bl * BLOCK:(bl + 1) * BLOCK, :].T

    tpos = t * tm + lax.broadcasted_iota(jnp.int32, (tm, 1), 0)
    for g, w in enumerate(POOL_WINDOWS):
        cs = slice(g * POOL_CH, (g + 1) * POOL_CH)
        ug = u[:, cs]
        acc = jnp.concatenate([halo_scr[:, cs], ug], axis=0)
        span = 1
        while span < w:
            acc = acc + pltpu.roll(acc, span, axis=0)
            span *= 2
        cnt = jnp.minimum(tpos + 1, w).astype(jnp.float32)
        d = acc[POOL_HALO:] / cnt - ug
        y = jnp.dot(d.astype(jnp.bfloat16), wpool_scr[g], preferred_element_type=jnp.float32)
        ypool_ref[0, :, cs] = (y * pscale_ref[:, cs]).astype(ypool_ref.dtype)
    halo_scr[...] = u[tm - POOL_HALO:tm, :]

    zv = jnp.dot(a, w_cols(POOL_WIDTH + 2 * ATTN_WIDTH, ATTN_WIDTH),
                 preferred_element_type=jnp.float32)
    for bl in range(nblk):
        vT_ref[0, bl] = zv[bl * BLOCK:(bl + 1) * BLOCK, :].T.astype(vT_ref.dtype)


def _bias_tile(tab_ref, h, d):
    r = lax.broadcasted_iota(jnp.int32, (BLOCK, BLOCK), 0)
    c = lax.broadcasted_iota(jnp.int32, (BLOCK, BLOCK), 1)
    n = d * BLOCK + c - r
    lo, hi = (0 if d == 0 else (d - 1) * BLOCK + 1), d * BLOCK + BLOCK - 1
    b0 = int(_BUCKET[lo])
    tile = jnp.full((BLOCK, BLOCK), tab_ref[b0, h] * LOG2E, jnp.float32)
    for b in range(b0 + 1, NUM_BUCKETS):
        if _THRESH[b] > hi:
            break
        tile = jnp.where(n >= _THRESH[b], tab_ref[b, h] * LOG2E, tile)
    if d == 0:
        tile = jnp.where(n >= 0, tile, NEG)
    return tile


def _fold8(x):
    return x.reshape(x.shape[0] // 8, 8, x.shape[1])


def _attn_items(nb):
    own = [(i, i // CHUNK) for i in range(nb)]
    prev = [(i, i // CHUNK - 1) for i in range(CHUNK, nb)]
    far = [(i, c) for i in range(nb) for c in range(i // CHUNK - 1)]
    return tuple(np.asarray(v, np.int32).T.copy() for v in (own, prev, far))


def _attn_kernel(tab_ref, own_ref, prev_ref, far_ref, qT_ref, kaug_ref, vT_ref, kbar_ref, oT_ref,
                 qaug_scr, s_buf, p_buf, m_st, acc_st, bias_scr):
    h = pl.program_id(0)
    b = pl.program_id(1)
    nb = kbar_ref.shape[1]

    @pl.when(b == 0)
    def _():
        for d in range(bias_scr.shape[0]):
            if d < NEAR:
                bias_scr[d] = _bias_tile(tab_ref, h, d)
            else:
                bias_scr[d] = jnp.zeros((BLOCK, BLOCK), jnp.float32)

    kb2 = kbar_ref[0, :, 0, :]
    kb = jnp.where(h % 2 == 0, kb2[:, :HEAD_DIM], kb2[:, HEAD_DIM:])
    kb_hi = kb.astype(jnp.bfloat16)
    kb_lo = (kb - kb_hi.astype(jnp.float32)).astype(jnp.bfloat16)
    jidx = lax.broadcasted_iota(jnp.int32, (nb, BLOCK), 0)
    cfar = jnp.full((nb, BLOCK), tab_ref[NUM_BUCKETS - 1, h] * LOG2E, jnp.float32)
    cfar_hi = cfar.astype(jnp.bfloat16).astype(jnp.float32)
    cfar_lo = cfar - cfar_hi

    def select(i):
        qT = qT_ref[0, i]
        q_hi = qT.astype(jnp.bfloat16)
        q_lo = (qT - q_hi.astype(jnp.float32)).astype(jnp.bfloat16)
        sc = (jnp.dot(kb_hi, q_hi, preferred_element_type=jnp.float32)
              + (jnp.dot(kb_hi, q_lo, preferred_element_type=jnp.float32)
                 + jnp.dot(kb_lo, q_hi, preferred_element_type=jnp.float32)))
        past = jidx < i
        work = jnp.where(past, sc, NEG)
        chosen = jnp.zeros((nb, BLOCK), jnp.bool_)
        for _ in range(TOP_K):
            mx = jnp.max(work, axis=0, keepdims=True)
            first = jnp.min(jnp.where(work == mx, jidx, nb), axis=0, keepdims=True)
            pick = jidx == first
            chosen = chosen | pick
            work = jnp.where(pick, -jnp.inf, work)
        sel = chosen & past
        far = jidx <= i - NEAR
        code_hi = jnp.where(past, jnp.where(sel, jnp.where(far, cfar_hi, 0.0), NEG),
                            jnp.where(jidx == i, 0.0, NEG))
        code_lo = jnp.where(sel & far, cfar_lo, 0.0)
        qaug_scr[i] = jnp.concatenate(
            [(qT * (HEAD_DIM ** -0.5 * LOG2E)).astype(jnp.bfloat16),
             code_hi.astype(jnp.bfloat16), code_lo.astype(jnp.bfloat16)], axis=0)

    def select_group(g, carry):
        for u in range(SELECT_UNROLL):
            select(g * SELECT_UNROLL + u)
        return carry

    lax.fori_loop(0, nb // SELECT_UNROLL, select_group, 0)

    def stage_qk(item, slot, near, nblk):
        i, c = item
        q_aug = qaug_scr[i]
        cm = None
        for w in range(nblk):
            row0 = pl.multiple_of((c * CHUNK + w) * BLOCK, BLOCK)
            sw = jnp.dot(kaug_ref[0, 0, pl.ds(row0, BLOCK), :], q_aug,
                         preferred_element_type=jnp.float32)
            if near:
                sw = sw + bias_scr[i - (c * CHUNK + w)]
            s_buf[slot, w] = sw
            fm = jnp.max(_fold8(sw), axis=0)
            cm = fm if cm is None else jnp.maximum(cm, fm)
        return cm

    def stage_exp(item, slot, cm, nblk, first):
        i, _ = item
        col_max = jnp.max(cm, axis=0, keepdims=True)
        if first:
            m_new = jnp.broadcast_to(col_max, cm.shape)
            alpha = None
        else:
            m_old = m_st[i]
            m_new = jnp.maximum(m_old, col_max)
            alpha = jnp.exp2(m_old - m_new)[0:1]
        m_st[i] = m_new
        m_row = m_new[0:1]
        for w in range(nblk):
            p = jnp.exp2(s_buf[slot, w] - m_row)
            p_buf[slot, w * BLOCK:(w + 1) * BLOCK, :] = p.astype(jnp.bfloat16)
        return alpha

    def ones_rows(keys):
        first = lax.broadcasted_iota(jnp.int32, (ACC_ROWS - HEAD_DIM, keys), 0) == 0
        return first.astype(jnp.bfloat16)

    def stage_pv(item, slot, alpha, nblk):
        i, c = item
        vblk = jnp.concatenate([vT_ref[0, c * CHUNK + w] for w in range(nblk)], axis=1)
        vblk = jnp.concatenate([vblk, ones_rows(nblk * BLOCK)], axis=0)
        pv = jnp.dot(vblk, p_buf[slot, :nblk * BLOCK, :],
                     preferred_element_type=jnp.float32)
        acc_st[i] = pv if alpha is None else alpha * acc_st[i] + pv

    def run(items_ref, near, lag, nblk_of_slot=lambda j: CHUNK, first=False):
        n = items_ref.shape[1]
        assert n % lag == 0 and n >= 2 * lag and lag <= s_buf.shape[0]
        item = lambda x: (items_ref[0, x], items_ref[1, x])

        def softmax_pv(x, j, cm):
            alpha = stage_exp(item(x), j, cm, nblk_of_slot(j), first)
            stage_pv(item(x), j, alpha, nblk_of_slot(j))

        cms = tuple(stage_qk(item(x), x, near, nblk_of_slot(x)) for x in range(lag))

        def body(k, cms):
            t = lag * k
            new = []
            for j in range(lag):
                softmax_pv(t - lag + j, j, cms[j])
                new.append(stage_qk(item(t + j), j, near, nblk_of_slot(j)))
            return tuple(new)

        cms = lax.fori_loop(1, n // lag, body, cms)
        for j in range(lag):
            softmax_pv(n - lag + j, j, cms[j])

    run(own_ref, True, LAG_OWN, lambda j: j % CHUNK + 1, first=True)
    run(prev_ref, True, LAG_PREV)
    run(far_ref, False, LAG_FAR)

    def finish(i, carry):
        acc = acc_st[i]
        oT_ref[0, i] = (acc[:HEAD_DIM] / acc[HEAD_DIM:HEAD_DIM + 1]).astype(oT_ref.dtype)
        return carry

    lax.fori_loop(0, nb, finish, 0)


def _post_kernel(x_ref, p_ref, ypool_ref, oT_ref, wop_ref, woa_ref, gpost_ref, gpre_ref,
                 gmpost_ref, wup_ref, wdown_ref, wgate_ref, wproj_ref, out_ref):
    nblk = oT_ref.shape[1]
    yat = jnp.concatenate([oT_ref[0, bl].T for bl in range(nblk)], axis=0)
    mix = (jnp.dot(ypool_ref[0], wop_ref[...], preferred_element_type=jnp.float32)
           + jnp.dot(yat.astype(jnp.bfloat16), woa_ref[...], preferred_element_type=jnp.float32))
    pe = jnp.dot(p_ref[0].astype(jnp.bfloat16), wproj_ref[...], preferred_element_type=jnp.float32)
    h1 = x_ref[0] + _rms(mix, gpost_ref[...])
    m = _rms(h1, gpre_ref[...]).astype(jnp.bfloat16)
    up = jnp.dot(m, wup_ref[...], preferred_element_type=jnp.float32)
    act = jnp.square(jnp.maximum(up, 0.0)).astype(jnp.bfloat16)
    f = jnp.dot(act, wdown_ref[...], preferred_element_type=jnp.float32)
    h2 = h1 + _rms(f, gmpost_ref[...])
    gate = jax.nn.sigmoid(jnp.dot(h2.astype(jnp.bfloat16), wgate_ref[...],
                                  preferred_element_type=jnp.float32))
    out_ref[0] = h2 + gate * pe


def _resident(shape):
    return pl.BlockSpec(shape, lambda *_: (0,) * len(shape), pipeline_mode=pl.Buffered(1))


def _mixers(x, w_in, w_pool, pool_scale, rel_bias, g_mix_pre, later_weights=()):
    B, S, _ = x.shape
    nb = S // BLOCK
    bf16 = jnp.bfloat16
    f32 = jnp.float32

    tm = TM_IN
    nblk = tm // BLOCK
    steps_b = S // tm
    slab = lambda w: pl.BlockSpec((w.shape[0] // (B * steps_b), w.shape[1]),
                                  lambda b, t: (b * steps_b + t, 0))
    ypool, qT, kaug, kbar, vT, *cast = pl.pallas_call(
        functools.partial(_inproj_kernel, len(later_weights)),
        grid=(B, steps_b),
        in_specs=[
            pl.BlockSpec((1, tm, D_MODEL), lambda b, t: (b, t, 0)),
            _resident((1, D_MODEL)),
            _resident(w_in.shape),
            _resident(w_pool.shape),
            _resident((1, POOL_WIDTH)),
            *[slab(w) for w in later_weights],
        ],
        out_specs=[
            pl.BlockSpec((1, tm, POOL_WIDTH), lambda b, t: (b, t, 0)),
            pl.BlockSpec((1, nblk, ATTN_WIDTH, BLOCK), lambda b, t: (b, t, 0, 0)),
            pl.BlockSpec((1, HEADS, tm, KAUG), lambda b, t: (b, 0, t, 0)),
            pl.BlockSpec((1, nblk, 1, ATTN_WIDTH), lambda b, t: (b, t, 0, 0)),
            pl.BlockSpec((1, nblk, ATTN_WIDTH, BLOCK), lambda b, t: (b, t, 0, 0)),
            *[slab(w) for w in later_weights],
        ],
        out_shape=[
            jax.ShapeDtypeStruct((B, S, POOL_WIDTH), bf16),
            jax.ShapeDtypeStruct((B, nb, ATTN_WIDTH, BLOCK), f32),
            jax.ShapeDtypeStruct((B, HEADS, S, KAUG), bf16),
            jax.ShapeDtypeStruct((B, nb, 1, ATTN_WIDTH), f32),
            jax.ShapeDtypeStruct((B, nb, ATTN_WIDTH, BLOCK), bf16),
            *[jax.ShapeDtypeStruct(w.shape, bf16) for w in later_weights],
        ],
        scratch_shapes=[pltpu.VMEM((POOL_HALO, POOL_WIDTH), f32),
                        pltpu.VMEM(w_in.shape, bf16), pltpu.VMEM(w_pool.shape, bf16)],
        compiler_params=pltpu.CompilerParams(
            dimension_semantics=("arbitrary", "arbitrary"), vmem_limit_bytes=VMEM_LIMIT),
        name="inproj_pool",
    )(x, g_mix_pre, w_in, w_pool, pool_scale, *later_weights)

    own_items, prev_items, far_items = _attn_items(nb)
    smem = pl.BlockSpec(memory_space=pltpu.SMEM)
    oT = pl.pallas_call(
        _attn_kernel,
        grid=(HEADS, B),
        in_specs=[
            smem, smem, smem, smem,
            pl.BlockSpec((1, nb, HEAD_DIM, BLOCK), lambda h, b: (b, 0, h, 0)),
            pl.BlockSpec((1, 1, S, KAUG), lambda h, b: (b, h, 0, 0)),
            pl.BlockSpec((1, nb, HEAD_DIM, BLOCK), lambda h, b: (b, 0, h, 0)),
            pl.BlockSpec((1, nb, 1, 2 * HEAD_DIM), lambda h, b: (b, 0, 0, h // 2)),
        ],
        out_specs=pl.BlockSpec((1, nb, HEAD_DIM, BLOCK), lambda h, b: (b, 0, h, 0)),
        out_shape=jax.ShapeDtypeStruct((B, nb, ATTN_WIDTH, BLOCK), f32),
        scratch_shapes=[
            pltpu.VMEM((nb, KAUG, BLOCK), bf16),
            pltpu.VMEM((LAG_MAX, CHUNK, BLOCK, BLOCK), f32),
            pltpu.VMEM((LAG_MAX, CHUNK_KEYS, BLOCK), bf16),
            pltpu.VMEM((nb, 8, BLOCK), f32),
            pltpu.VMEM((nb, ACC_ROWS, BLOCK), f32),
            pltpu.VMEM((2 * CHUNK, BLOCK, BLOCK), f32),
        ],
        compiler_params=pltpu.CompilerParams(
            dimension_semantics=("arbitrary", "arbitrary"), vmem_limit_bytes=VMEM_LIMIT),
        name="moba_attention",
    )(rel_bias, jnp.asarray(own_items), jnp.asarray(prev_items), jnp.asarray(far_items),
      qT, kaug, vT, kbar)
    return ypool, oT, cast


def _layer(x, p, w_in, w_pool, pool_scale, w_out, rel_bias, g_mix_pre, g_mix_post,
           g_mlp_pre, g_mlp_post, w_up, w_down, w_ple_proj, w_ple_gate):
    B, S, _ = x.shape
    bf16 = jnp.bfloat16
    f32 = jnp.float32
    ypool, oT, (wo, wup, wdown, wgate, wproj) = _mixers(
        x, w_in, w_pool, pool_scale, rel_bias, g_mix_pre,
        later_weights=(w_out, w_up, w_down, w_ple_gate, w_ple_proj))

    tm = TM_POST
    nblk = tm // BLOCK
    out = pl.pallas_call(
        _post_kernel,
        grid=(B, S // tm),
        in_specs=[
            pl.BlockSpec((1, tm, D_MODEL), lambda b, t: (b, t, 0)),
            pl.BlockSpec((1, tm, PLE_DIM), lambda b, t: (b, t, 0)),
            pl.BlockSpec((1, tm, POOL_WIDTH), lambda b, t: (b, t, 0)),
            pl.BlockSpec((1, nblk, ATTN_WIDTH, BLOCK), lambda b, t: (b, t, 0, 0)),
            pl.BlockSpec((POOL_WIDTH, D_MODEL), lambda b, t: (0, 0), pipeline_mode=pl.Buffered(1)),
            pl.BlockSpec((ATTN_WIDTH, D_MODEL), lambda b, t: (1, 0), pipeline_mode=pl.Buffered(1)),
            _resident((1, D_MODEL)),
            _resident((1, D_MODEL)),
            _resident((1, D_MODEL)),
            _resident((D_MODEL, D_FF)),
            _resident((D_FF, D_MODEL)),
            _resident((D_MODEL, D_MODEL)),
            _resident((PLE_DIM, D_MODEL)),
        ],
        out_specs=pl.BlockSpec((1, tm, D_MODEL), lambda b, t: (b, t, 0)),
        out_shape=jax.ShapeDtypeStruct((B, S, D_MODEL), f32),
        compiler_params=pltpu.CompilerParams(
            dimension_semantics=("arbitrary", "arbitrary"), vmem_limit_bytes=VMEM_LIMIT),
        name="post_mlp",
    )(x, p, ypool, oT, wo, wo, g_mix_post, g_mlp_pre, g_mlp_post, wup, wdown, wgate, wproj)
    return out


def kernel(x, p, w_in, w_pool, pool_scale, w_out, rel_bias, g_mix_pre, g_mix_post,
           g_mlp_pre, g_mlp_post, w_up, w_down, w_ple_proj, w_ple_gate):
    depth = w_in.shape[0]
    h = x
    for i in range(depth):
        h = _layer(h, p[i], w_in[i], w_pool[i], pool_scale[i:i + 1], w_out[i], rel_bias,
                   g_mix_pre[i:i + 1], g_mix_post[i:i + 1], g_mlp_pre[i:i + 1],
                   g_mlp_post[i:i + 1], w_up[i], w_down[i], w_ple_proj[i], w_ple_gate[i])
    return h
```

```python
import functools
import math

import numpy as np
import jax
import jax.numpy as jnp
from jax import lax
from jax.experimental import pallas as pl
from jax.experimental.pallas import tpu as pltpu

D_MODEL = 1024
POOL_WIDTH = 512
POOL_WINDOWS = (2, 4, 8, 16)
POOL_CH = 128
POOL_HALO = 16
ATTN_WIDTH = 512
HEAD_DIM = 64
HEADS = 8
BLOCK = 256
TOP_K = 3
NUM_BUCKETS = 32
MAX_DISTANCE = 1024
D_FF = 4096
PLE_DIM = 256
EPS = 1e-6
NEG = -1e30
LOG2E = math.log2(math.e)
KAUG = 128
NEAR = 5
CHUNK = 4
CHUNK_KEYS = CHUNK * BLOCK
ACC_ROWS = HEAD_DIM + 16
SELECT_UNROLL = 16
LAG_OWN = 8
LAG_PREV = 7
LAG_FAR = 12
LAG_MAX = max(LAG_OWN, LAG_PREV, LAG_FAR)

TM_IN = 1024
TM_POST = 512
VMEM_LIMIT = 56 * 1024 * 1024


def _bucket_thresholds():
    n = np.arange(0, NEAR * BLOCK + BLOCK, dtype=np.int64)
    max_exact = NUM_BUCKETS // 2
    nf = np.maximum(n, 1).astype(np.float64)
    large = max_exact + (np.log(nf / max_exact) / math.log(MAX_DISTANCE / max_exact)
                         * (NUM_BUCKETS - max_exact)).astype(np.int64)
    large = np.minimum(large, NUM_BUCKETS - 1)
    bucket = np.where(n < max_exact, n, large)
    assert np.all(np.diff(bucket) >= 0)
    return [int(np.argmax(bucket >= b)) for b in range(NUM_BUCKETS)], bucket


_THRESH, _BUCKET = _bucket_thresholds()
assert _BUCKET[(NEAR - 1) * BLOCK + 1] == NUM_BUCKETS - 1
assert NEAR <= CHUNK + 1
assert POOL_WIDTH == ATTN_WIDTH


def _rms(x, g):
    ms = jnp.mean(x * x, axis=-1, keepdims=True)
    return x * lax.rsqrt(ms + EPS) * g


def _inproj_kernel(n_cast, x_ref, g_ref, win_ref, wpool_ref, pscale_ref, *refs):
    cast_in, refs = refs[:n_cast], refs[n_cast:]
    ypool_ref, qT_ref, kaug_ref, kbar_ref, vT_ref = refs[:5]
    cast_out, (halo_scr, win_scr, wpool_scr) = refs[5:5 + n_cast], refs[5 + n_cast:]
    t = pl.program_id(1)

    @pl.when((pl.program_id(0) == 0) & (t == 0))
    def _():
        win_scr[...] = win_ref[...].astype(win_scr.dtype)
        wpool_scr[...] = wpool_ref[...].astype(wpool_scr.dtype)

    def w_cols(start, width):
        return win_scr[:, start:start + width]

    for src, dst in zip(cast_in, cast_out):
        dst[...] = src[...].astype(dst.dtype)
    tm = x_ref.shape[1]
    nblk = tm // BLOCK

    @pl.when(t == 0)
    def _():
        halo_scr[...] = jnp.zeros(halo_scr.shape, jnp.float32)

    a = _rms(x_ref[0], g_ref[...]).astype(jnp.bfloat16)

    zk = jnp.dot(a, w_cols(POOL_WIDTH + ATTN_WIDTH, ATTN_WIDTH),
                 preferred_element_type=jnp.float32)
    lane = lax.broadcasted_iota(jnp.int32, (tm, KAUG), 1)
    gblk = t * nblk + lax.broadcasted_iota(jnp.int32, (tm, KAUG), 0) // BLOCK
    onehot = jnp.where((lane == HEAD_DIM + gblk) | (lane == HEAD_DIM + 32 + gblk), 1.0, 0.0)
    for h in range(HEADS):
        pair = zk[:, (h // 2) * KAUG:(h // 2 + 1) * KAUG]
        if h % 2:
            pair = pltpu.roll(pair, HEAD_DIM, axis=1)
        kaug_ref[0, h] = jnp.where(lane < HEAD_DIM, pair, onehot).astype(kaug_ref.dtype)
    for bl in range(nblk):
        kbar_ref[0, bl] = jnp.mean(zk[bl * BLOCK:(bl + 1) * BLOCK, :], axis=0, keepdims=True)

    u = jnp.dot(a, w_cols(0, POOL_WIDTH), preferred_element_type=jnp.float32)

    zq = jnp.dot(a, w_cols(POOL_WIDTH, ATTN_WIDTH), preferred_element_type=jnp.float32)
    for bl in range(nblk):
        qT_ref[0, bl] = zq[bl * BLOCK:(bl + 1) * BLOCK, :].T

    tpos = t * tm + lax.broadcasted_iota(jnp.int32, (tm, 1), 0)
    for g, w in enumerate(POOL_WINDOWS):
        cs = slice(g * POOL_CH, (g + 1) * POOL_CH)
        ug = u[:, cs]
        acc = jnp.concatenate([halo_scr[:, cs], ug], axis=0)
        span = 1
        while span < w:
            acc = acc + pltpu.roll(acc, span, axis=0)
            span *= 2
        cnt = jnp.minimum(tpos + 1, w).astype(jnp.float32)
        d = acc[POOL_HALO:] / cnt - ug
        y = jnp.dot(d.astype(jnp.bfloat16), wpool_scr[g], preferred_element_type=jnp.float32)
        ypool_ref[0, :, cs] = (y * pscale_ref[:, cs]).astype(ypool_ref.dtype)
    halo_scr[...] = u[tm - POOL_HALO:tm, :]

    zv = jnp.dot(a, w_cols(POOL_WIDTH + 2 * ATTN_WIDTH, ATTN_WIDTH),
                 preferred_element_type=jnp.float32)
    for bl in range(nblk):
        vT_ref[0, bl] = zv[bl * BLOCK:(bl + 1) * BLOCK, :].T.astype(vT_ref.dtype)


def _bias_tile(tab_ref, h, d):
    r = lax.broadcasted_iota(jnp.int32, (BLOCK, BLOCK), 0)
    c = lax.broadcasted_iota(jnp.int32, (BLOCK, BLOCK), 1)
    n = d * BLOCK + c - r
    lo, hi = (0 if d == 0 else (d - 1) * BLOCK + 1), d * BLOCK + BLOCK - 1
    b0 = int(_BUCKET[lo])
    tile = jnp.full((BLOCK, BLOCK), tab_ref[b0, h] * LOG2E, jnp.float32)
    for b in range(b0 + 1, NUM_BUCKETS):
        if _THRESH[b] > hi:
            break
        tile = jnp.where(n >= _THRESH[b], tab_ref[b, h] * LOG2E, tile)
    if d == 0:
        tile = jnp.where(n >= 0, tile, NEG)
    return tile


def _fold8(x):
    return x.reshape(x.shape[0] // 8, 8, x.shape[1])


def _attn_items(nb):
    own = [(i, i // CHUNK) for i in range(nb)]
    prev = [(i, i // CHUNK - 1) for i in range(CHUNK, nb)]
    far = [(i, c) for i in range(nb) for c in range(i // CHUNK - 1)]
    return tuple(np.asarray(v, np.int32).T.copy() for v in (own, prev, far))


def _attn_kernel(tab_ref, own_ref, prev_ref, far_ref, qT_ref, kaug_ref, vT_ref, kbar_ref, oT_ref,
                 qaug_scr, s_buf, p_buf, m_st, acc_st, bias_scr):
    h = pl.program_id(0)
    b = pl.program_id(1)
    nb = kbar_ref.shape[1]

    @pl.when(b == 0)
    def _():
        for d in range(bias_scr.shape[0]):
            if d < NEAR:
                bias_scr[d] = _bias_tile(tab_ref, h, d)
            else:
                bias_scr[d] = jnp.zeros((BLOCK, BLOCK), jnp.float32)

    kb2 = kbar_ref[0, :, 0, :]
    kb = jnp.where(h % 2 == 0, kb2[:, :HEAD_DIM], kb2[:, HEAD_DIM:])
    kb_hi = kb.astype(jnp.bfloat16)
    kb_lo = (kb - kb_hi.astype(jnp.float32)).astype(jnp.bfloat16)
    jidx = lax.broadcasted_iota(jnp.int32, (nb, BLOCK), 0)
    cfar = jnp.full((nb, BLOCK), tab_ref[NUM_BUCKETS - 1, h] * LOG2E, jnp.float32)
    cfar_hi = cfar.astype(jnp.bfloat16).astype(jnp.float32)
    cfar_lo = cfar - cfar_hi

    def select(i):
        qT = qT_ref[0, i]
        q_hi = qT.astype(jnp.bfloat16)
        q_lo = (qT - q_hi.astype(jnp.float32)).astype(jnp.bfloat16)
        sc = (jnp.dot(kb_hi, q_hi, preferred_element_type=jnp.float32)
              + (jnp.dot(kb_hi, q_lo, preferred_element_type=jnp.float32)
                 + jnp.dot(kb_lo, q_hi, preferred_element_type=jnp.float32)))
        past = jidx < i
        work = jnp.where(past, sc, NEG)
        chosen = jnp.zeros((nb, BLOCK), jnp.bool_)
        for _ in range(TOP_K):
            mx = jnp.max(work, axis=0, keepdims=True)
            first = jnp.min(jnp.where(work == mx, jidx, nb), axis=0, keepdims=True)
            pick = jidx == first
            chosen = chosen | pick
            work = jnp.where(pick, -jnp.inf, work)
        sel = chosen & past
        far = jidx <= i - NEAR
        code_hi = jnp.where(past, jnp.where(sel, jnp.where(far, cfar_hi, 0.0), NEG),
                            jnp.where(jidx == i, 0.0, NEG))
        code_lo = jnp.where(sel & far, cfar_lo, 0.0)
        qaug_scr[i] = jnp.concatenate(
            [(qT * (HEAD_DIM ** -0.5 * LOG2E)).astype(jnp.bfloat16),
             code_hi.astype(jnp.bfloat16), code_lo.astype(jnp.bfloat16)], axis=0)

    def select_group(g, carry):
        for u in range(SELECT_UNROLL):
            select(g * SELECT_UNROLL + u)
        return carry

    n_groups = nb // SELECT_UNROLL
    lax.fori_loop(0, n_groups - 1, select_group, 0)

    def stage_qk(item, slot, near, nblk):
        i, c = item
        q_aug = qaug_scr[i]
        cm = None
        for w in range(nblk):
            row0 = pl.multiple_of((c * CHUNK + w) * BLOCK, BLOCK)
            sw = jnp.dot(kaug_ref[0, 0, pl.ds(row0, BLOCK), :], q_aug,
                         preferred_element_type=jnp.float32)
            if near:
                sw = sw + bias_scr[i - (c * CHUNK + w)]
            s_buf[slot, w] = sw
            fm = jnp.max(_fold8(sw), axis=0)
            cm = fm if cm is None else jnp.maximum(cm, fm)
        return cm

    def stage_exp(item, slot, cm, nblk, first):
        i, _ = item
        col_max = jnp.max(cm, axis=0, keepdims=True)
        if first:
            m_new = jnp.broadcast_to(col_max, cm.shape)
            alpha = None
        else:
            m_old = m_st[i]
            m_new = jnp.maximum(m_old, col_max)
            alpha = jnp.exp2(m_old - m_new)[0:1]
        m_st[i] = m_new
        m_row = m_new[0:1]
        for w in range(nblk):
            p = jnp.exp2(s_buf[slot, w] - m_row)
            p_buf[slot, w * BLOCK:(w + 1) * BLOCK, :] = p.astype(jnp.bfloat16)
        return alpha

    def ones_rows(keys):
        first = lax.broadcasted_iota(jnp.int32, (ACC_ROWS - HEAD_DIM, keys), 0) == 0
        return first.astype(jnp.bfloat16)

    def stage_pv(item, slot, alpha, nblk):
        i, c = item
        vblk = jnp.concatenate([vT_ref[0, c * CHUNK + w] for w in range(nblk)], axis=1)
        vblk = jnp.concatenate([vblk, ones_rows(nblk * BLOCK)], axis=0)
        pv = jnp.dot(vblk, p_buf[slot, :nblk * BLOCK, :],
                     preferred_element_type=jnp.float32)
        acc_st[i] = pv if alpha is None else alpha * acc_st[i] + pv

    def run(items_ref, near, lag, nblk_of_slot=lambda j: CHUNK, first=False,
            after_fill=lambda: None):
        n = items_ref.shape[1]
        assert n % lag == 0 and n >= 2 * lag and lag <= s_buf.shape[0]
        item = lambda x: (items_ref[0, x], items_ref[1, x])

        def softmax_pv(x, j, cm):
            alpha = stage_exp(item(x), j, cm, nblk_of_slot(j), first)
            stage_pv(item(x), j, alpha, nblk_of_slot(j))

        cms = tuple(stage_qk(item(x), x, near, nblk_of_slot(x)) for x in range(lag))
        after_fill()

        def body(k, cms):
            t = lag * k
            new = []
            for j in range(lag):
                softmax_pv(t - lag + j, j, cms[j])
                new.append(stage_qk(item(t + j), j, near, nblk_of_slot(j)))
            return tuple(new)

        cms = lax.fori_loop(1, n // lag, body, cms)
        for j in range(lag):
            softmax_pv(n - lag + j, j, cms[j])

    assert LAG_OWN <= (n_groups - 1) * SELECT_UNROLL
    run(own_ref, True, LAG_OWN, lambda j: j % CHUNK + 1, first=True,
        after_fill=lambda: select_group(n_groups - 1, 0))
    run(prev_ref, True, LAG_PREV)
    run(far_ref, False, LAG_FAR)

    def finish(i, carry):
        acc = acc_st[i]
        oT_ref[0, i] = (acc[:HEAD_DIM] / acc[HEAD_DIM:HEAD_DIM + 1]).astype(oT_ref.dtype)
        return carry

    lax.fori_loop(0, nb, finish, 0)


def _post_kernel(x_ref, p_ref, ypool_ref, oT_ref, wop_ref, woa_ref, gpost_ref, gpre_ref,
                 gmpost_ref, wup_ref, wdown_ref, wgate_ref, wproj_ref, out_ref):
    nblk = oT_ref.shape[1]
    yat = jnp.concatenate([oT_ref[0, bl].T for bl in range(nblk)], axis=0)
    mix = (jnp.dot(ypool_ref[0], wop_ref[...], preferred_element_type=jnp.float32)
           + jnp.dot(yat.astype(jnp.bfloat16), woa_ref[...], preferred_element_type=jnp.float32))
    pe = jnp.dot(p_ref[0].astype(jnp.bfloat16), wproj_ref[...], preferred_element_type=jnp.float32)
    h1 = x_ref[0] + _rms(mix, gpost_ref[...])
    m = _rms(h1, gpre_ref[...]).astype(jnp.bfloat16)
    up = jnp.dot(m, wup_ref[...], preferred_element_type=jnp.float32)
    act = jnp.square(jnp.maximum(up, 0.0)).astype(jnp.bfloat16)
    f = jnp.dot(act, wdown_ref[...], preferred_element_type=jnp.float32)
    h2 = h1 + _rms(f, gmpost_ref[...])
    gate = jax.nn.sigmoid(jnp.dot(h2.astype(jnp.bfloat16), wgate_ref[...],
                                  preferred_element_type=jnp.float32))
    out_ref[0] = h2 + gate * pe


def _resident(shape):
    return pl.BlockSpec(shape, lambda *_: (0,) * len(shape), pipeline_mode=pl.Buffered(1))


def _mixers(x, w_in, w_pool, pool_scale, rel_bias, g_mix_pre, later_weights=()):
    B, S, _ = x.shape
    nb = S // BLOCK
    bf16 = jnp.bfloat16
    f32 = jnp.float32

    tm = TM_IN
    nblk = tm // BLOCK
    steps_b = S // tm
    slab = lambda w: pl.BlockSpec((w.shape[0] // (B * steps_b), w.shape[1]),
                                  lambda b, t: (b * steps_b + t, 0))
    ypool, qT, kaug, kbar, vT, *cast = pl.pallas_call(
        functools.partial(_inproj_kernel, len(later_weights)),
        grid=(B, steps_b),
        in_specs=[
            pl.BlockSpec((1, tm, D_MODEL), lambda b, t: (b, t, 0)),
            _resident((1, D_MODEL)),
            _resident(w_in.shape),
            _resident(w_pool.shape),
            _resident((1, POOL_WIDTH)),
            *[slab(w) for w in later_weights],
        ],
        out_specs=[
            pl.BlockSpec((1, tm, POOL_WIDTH), lambda b, t: (b, t, 0)),
            pl.BlockSpec((1, nblk, ATTN_WIDTH, BLOCK), lambda b, t: (b, t, 0, 0)),
            pl.BlockSpec((1, HEADS, tm, KAUG), lambda b, t: (b, 0, t, 0)),
            pl.BlockSpec((1, nblk, 1, ATTN_WIDTH), lambda b, t: (b, t, 0, 0)),
            pl.BlockSpec((1, nblk, ATTN_WIDTH, BLOCK), lambda b, t: (b, t, 0, 0)),
            *[slab(w) for w in later_weights],
        ],
        out_shape=[
            jax.ShapeDtypeStruct((B, S, POOL_WIDTH), bf16),
            jax.ShapeDtypeStruct((B, nb, ATTN_WIDTH, BLOCK), f32),
            jax.ShapeDtypeStruct((B, HEADS, S, KAUG), bf16),
            jax.ShapeDtypeStruct((B, nb, 1, ATTN_WIDTH), f32),
            jax.ShapeDtypeStruct((B, nb, ATTN_WIDTH, BLOCK), bf16),
            *[jax.ShapeDtypeStruct(w.shape, bf16) for w in later_weights],
        ],
        scratch_shapes=[pltpu.VMEM((POOL_HALO, POOL_WIDTH), f32),
                        pltpu.VMEM(w_in.shape, bf16), pltpu.VMEM(w_pool.shape, bf16)],
        compiler_params=pltpu.CompilerParams(
            dimension_semantics=("arbitrary", "arbitrary"), vmem_limit_bytes=VMEM_LIMIT),
        name="inproj_pool",
    )(x, g_mix_pre, w_in, w_pool, pool_scale, *later_weights)

    own_items, prev_items, far_items = _attn_items(nb)
    smem = pl.BlockSpec(memory_space=pltpu.SMEM)
    oT = pl.pallas_call(
        _attn_kernel,
        grid=(HEADS, B),
        in_specs=[
            smem, smem, smem, smem,
            pl.BlockSpec((1, nb, HEAD_DIM, BLOCK), lambda h, b: (b, 0, h, 0)),
            pl.BlockSpec((1, 1, S, KAUG), lambda h, b: (b, h, 0, 0)),
            pl.BlockSpec((1, nb, HEAD_DIM, BLOCK), lambda h, b: (b, 0, h, 0)),
            pl.BlockSpec((1, nb, 1, 2 * HEAD_DIM), lambda h, b: (b, 0, 0, h // 2)),
        ],
        out_specs=pl.BlockSpec((1, nb, HEAD_DIM, BLOCK), lambda h, b: (b, 0, h, 0)),
        out_shape=jax.ShapeDtypeStruct((B, nb, ATTN_WIDTH, BLOCK), f32),
        scratch_shapes=[
            pltpu.VMEM((nb, KAUG, BLOCK), bf16),
            pltpu.VMEM((LAG_MAX, CHUNK, BLOCK, BLOCK), f32),
            pltpu.VMEM((LAG_MAX, CHUNK_KEYS, BLOCK), bf16),
            pltpu.VMEM((nb, 8, BLOCK), f32),
            pltpu.VMEM((nb, ACC_ROWS, BLOCK), f32),
            pltpu.VMEM((2 * CHUNK, BLOCK, BLOCK), f32),
        ],
        compiler_params=pltpu.CompilerParams(
            dimension_semantics=("arbitrary", "arbitrary"), vmem_limit_bytes=VMEM_LIMIT),
        name="moba_attention",
    )(rel_bias, jnp.asarray(own_items), jnp.asarray(prev_items), jnp.asarray(far_items),
      qT, kaug, vT, kbar)
    return ypool, oT, cast


def _layer(x, p, w_in, w_pool, pool_scale, w_out, rel_bias, g_mix_pre, g_mix_post,
           g_mlp_pre, g_mlp_post, w_up, w_down, w_ple_proj, w_ple_gate):
    B, S, _ = x.shape
    bf16 = jnp.bfloat16
    f32 = jnp.float32
    ypool, oT, (wo, wup, wdown, wgate, wproj) = _mixers(
        x, w_in, w_pool, pool_scale, rel_bias, g_mix_pre,
        later_weights=(w_out, w_up, w_down, w_ple_gate, w_ple_proj))

    tm = TM_POST
    nblk = tm // BLOCK
    out = pl.pallas_call(
        _post_kernel,
        grid=(B, S // tm),
        in_specs=[
            pl.BlockSpec((1, tm, D_MODEL), lambda b, t: (b, t, 0)),
            pl.BlockSpec((1, tm, PLE_DIM), lambda b, t: (b, t, 0)),
            pl.BlockSpec((1, tm, POOL_WIDTH), lambda b, t: (b, t, 0)),
            pl.BlockSpec((1, nblk, ATTN_WIDTH, BLOCK), lambda b, t: (b, t, 0, 0)),
            pl.BlockSpec((POOL_WIDTH, D_MODEL), lambda b, t: (0, 0), pipeline_mode=pl.Buffered(1)),
            pl.BlockSpec((ATTN_WIDTH, D_MODEL), lambda b, t: (1, 0), pipeline_mode=pl.Buffered(1)),
            _resident((1, D_MODEL)),
            _resident((1, D_MODEL)),
            _resident((1, D_MODEL)),
            _resident((D_MODEL, D_FF)),
            _resident((D_FF, D_MODEL)),
            _resident((D_MODEL, D_MODEL)),
            _resident((PLE_DIM, D_MODEL)),
        ],
        out_specs=pl.BlockSpec((1, tm, D_MODEL), lambda b, t: (b, t, 0)),
        out_shape=jax.ShapeDtypeStruct((B, S, D_MODEL), f32),
        compiler_params=pltpu.CompilerParams(
            dimension_semantics=("arbitrary", "arbitrary"), vmem_limit_bytes=VMEM_LIMIT),
        name="post_mlp",
    )(x, p, ypool, oT, wo, wo, g_mix_post, g_mlp_pre, g_mlp_post, wup, wdown, wgate, wproj)
    return out


def kernel(x, p, w_in, w_pool, pool_scale, w_out, rel_bias, g_mix_pre, g_mix_post,
           g_mlp_pre, g_mlp_post, w_up, w_down, w_ple_proj, w_ple_gate):
    depth = w_in.shape[0]
    h = x
    for i in range(depth):
        h = _layer(h, p[i], w_in[i], w_pool[i], pool_scale[i:i + 1], w_out[i], rel_bias,
                   g_mix_pre[i:i + 1], g_mix_post[i:i + 1], g_mlp_pre[i:i + 1],
                   g_mlp_post[i:i + 1], w_up[i], w_down[i], w_ple_proj[i], w_ple_gate[i])
    return h
```

```python
import functools
import math

import numpy as np
import jax
import jax.numpy as jnp
from jax import lax
from jax.experimental import pallas as pl
from jax.experimental.pallas import tpu as pltpu

D_MODEL = 1024
POOL_WIDTH = 512
POOL_WINDOWS = (2, 4, 8, 16)
POOL_CH = 128
POOL_HALO = 16
ATTN_WIDTH = 512
HEAD_DIM = 64
HEADS = 8
BLOCK = 256
TOP_K = 3
NUM_BUCKETS = 32
MAX_DISTANCE = 1024
D_FF = 4096
PLE_DIM = 256
EPS = 1e-6
NEG = -1e30
LOG2E = math.log2(math.e)
KAUG = 128
NEAR = 5
CHUNK = 4
CHUNK_KEYS = CHUNK * BLOCK
ACC_ROWS = HEAD_DIM + 16
SELECT_UNROLL = 16
LAG_OWN = 8
LAG_PAST = 16
LAG_MAX = max(LAG_OWN, LAG_PAST)

TM_IN = 1024
TM_POST = 512
VMEM_LIMIT = 56 * 1024 * 1024


def _bucket_thresholds():
    n = np.arange(0, NEAR * BLOCK + BLOCK, dtype=np.int64)
    max_exact = NUM_BUCKETS // 2
    nf = np.maximum(n, 1).astype(np.float64)
    large = max_exact + (np.log(nf / max_exact) / math.log(MAX_DISTANCE / max_exact)
                         * (NUM_BUCKETS - max_exact)).astype(np.int64)
    large = np.minimum(large, NUM_BUCKETS - 1)
    bucket = np.where(n < max_exact, n, large)
    assert np.all(np.diff(bucket) >= 0)
    return [int(np.argmax(bucket >= b)) for b in range(NUM_BUCKETS)], bucket


_THRESH, _BUCKET = _bucket_thresholds()
assert _BUCKET[(NEAR - 1) * BLOCK + 1] == NUM_BUCKETS - 1
assert NEAR <= CHUNK + 1
assert POOL_WIDTH == ATTN_WIDTH


def _rms(x, g):
    ms = jnp.mean(x * x, axis=-1, keepdims=True)
    return x * lax.rsqrt(ms + EPS) * g


def _inproj_kernel(n_cast, x_ref, g_ref, win_ref, wpool_ref, pscale_ref, *refs):
    cast_in, refs = refs[:n_cast], refs[n_cast:]
    ypool_ref, qT_ref, kaug_ref, kbar_ref, vT_ref = refs[:5]
    cast_out, (halo_scr, win_scr, wpool_scr) = refs[5:5 + n_cast], refs[5 + n_cast:]
    t = pl.program_id(1)

    @pl.when((pl.program_id(0) == 0) & (t == 0))
    def _():
        win_scr[...] = win_ref[...].astype(win_scr.dtype)
        wpool_scr[...] = wpool_ref[...].astype(wpool_scr.dtype)

    def w_cols(start, width):
        return win_scr[:, start:start + width]

    for src, dst in zip(cast_in, cast_out):
        dst[...] = src[...].astype(dst.dtype)
    tm = x_ref.shape[1]
    nblk = tm // BLOCK

    @pl.when(t == 0)
    def _():
        halo_scr[...] = jnp.zeros(halo_scr.shape, jnp.float32)

    a = _rms(x_ref[0], g_ref[...]).astype(jnp.bfloat16)

    zk = jnp.dot(a, w_cols(POOL_WIDTH + ATTN_WIDTH, ATTN_WIDTH),
                 preferred_element_type=jnp.float32)
    lane = lax.broadcasted_iota(jnp.int32, (tm, KAUG), 1)
    gblk = t * nblk + lax.broadcasted_iota(jnp.int32, (tm, KAUG), 0) // BLOCK
    onehot = jnp.where((lane == HEAD_DIM + gblk) | (lane == HEAD_DIM + 32 + gblk), 1.0, 0.0)
    for h in range(HEADS):
        pair = zk[:, (h // 2) * KAUG:(h // 2 + 1) * KAUG]
        if h % 2:
            pair = pltpu.roll(pair, HEAD_DIM, axis=1)
        kaug_ref[0, h] = jnp.where(lane < HEAD_DIM, pair, onehot).astype(kaug_ref.dtype)
    for bl in range(nblk):
        kbar_ref[0, bl] = jnp.mean(zk[bl * BLOCK:(bl + 1) * BLOCK, :], axis=0, keepdims=True)

    u = jnp.dot(a, w_cols(0, POOL_WIDTH), preferred_element_type=jnp.float32)

    zq = jnp.dot(a, w_cols(POOL_WIDTH, ATTN_WIDTH), preferred_element_type=jnp.float32)
    for bl in range(nblk):
        qT_ref[0, bl] = zq[bl * BLOCK:(bl + 1) * BLOCK, :].T

    tpos = t * tm + lax.broadcasted_iota(jnp.int32, (tm, 1), 0)
    for g, w in enumerate(POOL_WINDOWS):
        cs = slice(g * POOL_CH, (g + 1) * POOL_CH)
        ug = u[:, cs]
        acc = jnp.concatenate([halo_scr[:, cs], ug], axis=0)
        span = 1
        while span < w:
            acc = acc + pltpu.roll(acc, span, axis=0)
            span *= 2
        cnt = jnp.minimum(tpos + 1, w).astype(jnp.float32)
        d = acc[POOL_HALO:] / cnt - ug
        y = jnp.dot(d.astype(jnp.bfloat16), wpool_scr[g], preferred_element_type=jnp.float32)
        ypool_ref[0, :, cs] = (y * pscale_ref[:, cs]).astype(ypool_ref.dtype)
    halo_scr[...] = u[tm - POOL_HALO:tm, :]

    zv = jnp.dot(a, w_cols(POOL_WIDTH + 2 * ATTN_WIDTH, ATTN_WIDTH),
                 preferred_element_type=jnp.float32)
    for bl in range(nblk):
        vT_ref[0, bl] = zv[bl * BLOCK:(bl + 1) * BLOCK, :].T.astype(vT_ref.dtype)


def _bias_tile(tab_ref, h, d):
    r = lax.broadcasted_iota(jnp.int32, (BLOCK, BLOCK), 0)
    c = lax.broadcasted_iota(jnp.int32, (BLOCK, BLOCK), 1)
    n = d * BLOCK + c - r
    lo, hi = (0 if d == 0 else (d - 1) * BLOCK + 1), d * BLOCK + BLOCK - 1
    b0 = int(_BUCKET[lo])
    tile = jnp.full((BLOCK, BLOCK), tab_ref[b0, h] * LOG2E, jnp.float32)
    for b in range(b0 + 1, NUM_BUCKETS):
        if _THRESH[b] > hi:
            break
        tile = jnp.where(n >= _THRESH[b], tab_ref[b, h] * LOG2E, tile)
    if d == 0:
        tile = jnp.where(n >= 0, tile, NEG)
    return tile


def _fold8(x):
    return x.reshape(x.shape[0] // 8, 8, x.shape[1])


def _attn_items(nb):
    own = [(i, i // CHUNK) for i in range(nb)]
    past = [(i, c) for i in range(nb) for c in range(i // CHUNK)]
    return tuple(np.asarray(v, np.int32).T.copy() for v in (own, past))


def _attn_kernel(tab_ref, own_ref, past_ref, qT_ref, kaug_ref, vT_ref, kbar_ref, oT_ref,
                 qaug_scr, s_buf, p_buf, m_st, acc_st, bias_scr):
    h = pl.program_id(0)
    b = pl.program_id(1)
    nb = kbar_ref.shape[1]

    @pl.when(b == 0)
    def _():
        for d in range(bias_scr.shape[0]):
            if d < NEAR:
                bias_scr[d] = _bias_tile(tab_ref, h, d)
            else:
                bias_scr[d] = jnp.zeros((BLOCK, BLOCK), jnp.float32)

    kb2 = kbar_ref[0, :, 0, :]
    kb = jnp.where(h % 2 == 0, kb2[:, :HEAD_DIM], kb2[:, HEAD_DIM:])
    kb_hi = kb.astype(jnp.bfloat16)
    kb_lo = (kb - kb_hi.astype(jnp.float32)).astype(jnp.bfloat16)
    jidx = lax.broadcasted_iota(jnp.int32, (nb, BLOCK), 0)
    cfar = jnp.full((nb, BLOCK), tab_ref[NUM_BUCKETS - 1, h] * LOG2E, jnp.float32)
    cfar_hi = cfar.astype(jnp.bfloat16).astype(jnp.float32)
    cfar_lo = cfar - cfar_hi

    def select(i):
        qT = qT_ref[0, i]
        q_hi = qT.astype(jnp.bfloat16)
        q_lo = (qT - q_hi.astype(jnp.float32)).astype(jnp.bfloat16)
        sc = (jnp.dot(kb_hi, q_hi, preferred_element_type=jnp.float32)
              + (jnp.dot(kb_hi, q_lo, preferred_element_type=jnp.float32)
                 + jnp.dot(kb_lo, q_hi, preferred_element_type=jnp.float32)))
        past = jidx < i
        work = jnp.where(past, sc, NEG)
        chosen = jnp.zeros((nb, BLOCK), jnp.bool_)
        for _ in range(TOP_K):
            mx = jnp.max(work, axis=0, keepdims=True)
            first = jnp.min(jnp.where(work == mx, jidx, nb), axis=0, keepdims=True)
            pick = jidx == first
            chosen = chosen | pick
            work = jnp.where(pick, -jnp.inf, work)
        sel = chosen & past
        far = jidx <= i - NEAR
        code_hi = jnp.where(past, jnp.where(sel, jnp.where(far, cfar_hi, 0.0), NEG),
                            jnp.where(jidx == i, 0.0, NEG))
        code_lo = jnp.where(sel & far, cfar_lo, 0.0)
        qaug_scr[i] = jnp.concatenate(
            [(qT * (HEAD_DIM ** -0.5 * LOG2E)).astype(jnp.bfloat16),
             code_hi.astype(jnp.bfloat16), code_lo.astype(jnp.bfloat16)], axis=0)

    def select_group(g, carry):
        for u in range(SELECT_UNROLL):
            select(g * SELECT_UNROLL + u)
        return carry

    n_groups = nb // SELECT_UNROLL
    lax.fori_loop(0, n_groups - 1, select_group, 0)

    def stage_qk(item, slot, near, nblk):
        i, c = item
        q_aug = qaug_scr[i]
        cm = None
        for w in range(nblk):
            row0 = pl.multiple_of((c * CHUNK + w) * BLOCK, BLOCK)
            sw = jnp.dot(kaug_ref[0, 0, pl.ds(row0, BLOCK), :], q_aug,
                         preferred_element_type=jnp.float32)
            if near:
                sw = sw + bias_scr[jnp.minimum(i - (c * CHUNK + w), bias_scr.shape[0] - 1)]
            s_buf[slot, w] = sw
            fm = jnp.max(_fold8(sw), axis=0)
            cm = fm if cm is None else jnp.maximum(cm, fm)
        return cm

    def stage_exp(item, slot, cm, nblk, first):
        i, _ = item
        col_max = jnp.max(cm, axis=0, keepdims=True)
        if first:
            m_new = jnp.broadcast_to(col_max, cm.shape)
            alpha = None
        else:
            m_old = m_st[i]
            m_new = jnp.maximum(m_old, col_max)
            alpha = jnp.exp2(m_old - m_new)[0:1]
        m_st[i] = m_new
        m_row = m_new[0:1]
        for w in range(nblk):
            p = jnp.exp2(s_buf[slot, w] - m_row)
            p_buf[slot, w * BLOCK:(w + 1) * BLOCK, :] = p.astype(jnp.bfloat16)
        return alpha

    def ones_rows(keys):
        first = lax.broadcasted_iota(jnp.int32, (ACC_ROWS - HEAD_DIM, keys), 0) == 0
        return first.astype(jnp.bfloat16)

    def stage_pv(item, slot, alpha, nblk):
        i, c = item
        vblk = jnp.concatenate([vT_ref[0, c * CHUNK + w] for w in range(nblk)], axis=1)
        vblk = jnp.concatenate([vblk, ones_rows(nblk * BLOCK)], axis=0)
        pv = jnp.dot(vblk, p_buf[slot, :nblk * BLOCK, :],
                     preferred_element_type=jnp.float32)
        acc_st[i] = pv if alpha is None else alpha * acc_st[i] + pv

    def run(items_ref, near, lag, nblk_of_slot=lambda j: CHUNK, first=False,
            after_fill=lambda: None):
        n = items_ref.shape[1]
        assert n % lag == 0 and n >= 2 * lag and lag <= s_buf.shape[0]
        item = lambda x: (items_ref[0, x], items_ref[1, x])

        def softmax_pv(x, j, cm):
            alpha = stage_exp(item(x), j, cm, nblk_of_slot(j), first)
            stage_pv(item(x), j, alpha, nblk_of_slot(j))

        cms = tuple(stage_qk(item(x), x, near, nblk_of_slot(x)) for x in range(lag))
        after_fill()

        def body(k, cms):
            t = lag * k
            new = []
            for j in range(lag):
                softmax_pv(t - lag + j, j, cms[j])
                new.append(stage_qk(item(t + j), j, near, nblk_of_slot(j)))
            return tuple(new)

        cms = lax.fori_loop(1, n // lag, body, cms)
        for j in range(lag):
            softmax_pv(n - lag + j, j, cms[j])

    assert LAG_OWN <= (n_groups - 1) * SELECT_UNROLL
    run(own_ref, True, LAG_OWN, lambda j: j % CHUNK + 1, first=True,
        after_fill=lambda: select_group(n_groups - 1, 0))
    run(past_ref, True, LAG_PAST)

    def finish(i, carry):
        acc = acc_st[i]
        oT_ref[0, i] = (acc[:HEAD_DIM] / acc[HEAD_DIM:HEAD_DIM + 1]).astype(oT_ref.dtype)
        return carry

    lax.fori_loop(0, nb, finish, 0)


def _post_kernel(x_ref, p_ref, ypool_ref, oT_ref, wop_ref, woa_ref, gpost_ref, gpre_ref,
                 gmpost_ref, wup_ref, wdown_ref, wgate_ref, wproj_ref, out_ref):
    nblk = oT_ref.shape[1]
    yat = jnp.concatenate([oT_ref[0, bl].T for bl in range(nblk)], axis=0)
    mix = (jnp.dot(ypool_ref[0], wop_ref[...], preferred_element_type=jnp.float32)
           + jnp.dot(yat.astype(jnp.bfloat16), woa_ref[...], preferred_element_type=jnp.float32))
    pe = jnp.dot(p_ref[0].astype(jnp.bfloat16), wproj_ref[...], preferred_element_type=jnp.float32)
    h1 = x_ref[0] + _rms(mix, gpost_ref[...])
    m = _rms(h1, gpre_ref[...]).astype(jnp.bfloat16)
    up = jnp.dot(m, wup_ref[...], preferred_element_type=jnp.float32)
    act = jnp.square(jnp.maximum(up, 0.0)).astype(jnp.bfloat16)
    f = jnp.dot(act, wdown_ref[...], preferred_element_type=jnp.float32)
    h2 = h1 + _rms(f, gmpost_ref[...])
    gate = jax.nn.sigmoid(jnp.dot(h2.astype(jnp.bfloat16), wgate_ref[...],
                                  preferred_element_type=jnp.float32))
    out_ref[0] = h2 + gate * pe


def _resident(shape):
    return pl.BlockSpec(shape, lambda *_: (0,) * len(shape), pipeline_mode=pl.Buffered(1))


def _mixers(x, w_in, w_pool, pool_scale, rel_bias, g_mix_pre, later_weights=()):
    B, S, _ = x.shape
    nb = S // BLOCK
    bf16 = jnp.bfloat16
    f32 = jnp.float32

    tm = TM_IN
    nblk = tm // BLOCK
    steps_b = S // tm
    slab = lambda w: pl.BlockSpec((w.shape[0] // (B * steps_b), w.shape[1]),
                                  lambda b, t: (b * steps_b + t, 0))
    ypool, qT, kaug, kbar, vT, *cast = pl.pallas_call(
        functools.partial(_inproj_kernel, len(later_weights)),
        grid=(B, steps_b),
        in_specs=[
            pl.BlockSpec((1, tm, D_MODEL), lambda b, t: (b, t, 0)),
            _resident((1, D_MODEL)),
            _resident(w_in.shape),
            _resident(w_pool.shape),
            _resident((1, POOL_WIDTH)),
            *[slab(w) for w in later_weights],
        ],
        out_specs=[
            pl.BlockSpec((1, tm, POOL_WIDTH), lambda b, t: (b, t, 0)),
            pl.BlockSpec((1, nblk, ATTN_WIDTH, BLOCK), lambda b, t: (b, t, 0, 0)),
            pl.BlockSpec((1, HEADS, tm, KAUG), lambda b, t: (b, 0, t, 0)),
            pl.BlockSpec((1, nblk, 1, ATTN_WIDTH), lambda b, t: (b, t, 0, 0)),
            pl.BlockSpec((1, nblk, ATTN_WIDTH, BLOCK), lambda b, t: (b, t, 0, 0)),
            *[slab(w) for w in later_weights],
        ],
        out_shape=[
            jax.ShapeDtypeStruct((B, S, POOL_WIDTH), bf16),
            jax.ShapeDtypeStruct((B, nb, ATTN_WIDTH, BLOCK), f32),
            jax.ShapeDtypeStruct((B, HEADS, S, KAUG), bf16),
            jax.ShapeDtypeStruct((B, nb, 1, ATTN_WIDTH), f32),
            jax.ShapeDtypeStruct((B, nb, ATTN_WIDTH, BLOCK), bf16),
            *[jax.ShapeDtypeStruct(w.shape, bf16) for w in later_weights],
        ],
        scratch_shapes=[pltpu.VMEM((POOL_HALO, POOL_WIDTH), f32),
                        pltpu.VMEM(w_in.shape, bf16), pltpu.VMEM(w_pool.shape, bf16)],
        compiler_params=pltpu.CompilerParams(
            dimension_semantics=("arbitrary", "arbitrary"), vmem_limit_bytes=VMEM_LIMIT),
        name="inproj_pool",
    )(x, g_mix_pre, w_in, w_pool, pool_scale, *later_weights)

    own_items, past_items = _attn_items(nb)
    smem = pl.BlockSpec(memory_space=pltpu.SMEM)
    oT = pl.pallas_call(
        _attn_kernel,
        grid=(HEADS, B),
        in_specs=[
            smem, smem, smem,
            pl.BlockSpec((1, nb, HEAD_DIM, BLOCK), lambda h, b: (b, 0, h, 0)),
            pl.BlockSpec((1, 1, S, KAUG), lambda h, b: (b, h, 0, 0)),
            pl.BlockSpec((1, nb, HEAD_DIM, BLOCK), lambda h, b: (b, 0, h, 0)),
            pl.BlockSpec((1, nb, 1, 2 * HEAD_DIM), lambda h, b: (b, 0, 0, h // 2)),
        ],
        out_specs=pl.BlockSpec((1, nb, HEAD_DIM, BLOCK), lambda h, b: (b, 0, h, 0)),
        out_shape=jax.ShapeDtypeStruct((B, nb, ATTN_WIDTH, BLOCK), f32),
        scratch_shapes=[
            pltpu.VMEM((nb, KAUG, BLOCK), bf16),
            pltpu.VMEM((LAG_MAX, CHUNK, BLOCK, BLOCK), f32),
            pltpu.VMEM((LAG_MAX, CHUNK_KEYS, BLOCK), bf16),
            pltpu.VMEM((nb, 8, BLOCK), f32),
            pltpu.VMEM((nb, ACC_ROWS, BLOCK), f32),
            pltpu.VMEM((2 * CHUNK, BLOCK, BLOCK), f32),
        ],
        compiler_params=pltpu.CompilerParams(
            dimension_semantics=("arbitrary", "arbitrary"), vmem_limit_bytes=VMEM_LIMIT),
        name="moba_attention",
    )(rel_bias, jnp.asarray(own_items), jnp.asarray(past_items), qT, kaug, vT, kbar)
    return ypool, oT, cast


def _layer(x, p, w_in, w_pool, pool_scale, w_out, rel_bias, g_mix_pre, g_mix_post,
           g_mlp_pre, g_mlp_post, w_up, w_down, w_ple_proj, w_ple_gate):
    B, S, _ = x.shape
    bf16 = jnp.bfloat16
    f32 = jnp.float32
    ypool, oT, (wo, wup, wdown, wgate, wproj) = _mixers(
        x, w_in, w_pool, pool_scale, rel_bias, g_mix_pre,
        later_weights=(w_out, w_up, w_down, w_ple_gate, w_ple_proj))

    tm = TM_POST
    nblk = tm // BLOCK
    out = pl.pallas_call(
        _post_kernel,
        grid=(B, S // tm),
        in_specs=[
            pl.BlockSpec((1, tm, D_MODEL), lambda b, t: (b, t, 0)),
            pl.BlockSpec((1, tm, PLE_DIM), lambda b, t: (b, t, 0)),
            pl.BlockSpec((1, tm, POOL_WIDTH), lambda b, t: (b, t, 0)),
            pl.BlockSpec((1, nblk, ATTN_WIDTH, BLOCK), lambda b, t: (b, t, 0, 0)),
            pl.BlockSpec((POOL_WIDTH, D_MODEL), lambda b, t: (0, 0), pipeline_mode=pl.Buffered(1)),
            pl.BlockSpec((ATTN_WIDTH, D_MODEL), lambda b, t: (1, 0), pipeline_mode=pl.Buffered(1)),
            _resident((1, D_MODEL)),
            _resident((1, D_MODEL)),
            _resident((1, D_MODEL)),
            _resident((D_MODEL, D_FF)),
            _resident((D_FF, D_MODEL)),
            _resident((D_MODEL, D_MODEL)),
            _resident((PLE_DIM, D_MODEL)),
        ],
        out_specs=pl.BlockSpec((1, tm, D_MODEL), lambda b, t: (b, t, 0)),
        out_shape=jax.ShapeDtypeStruct((B, S, D_MODEL), f32),
        compiler_params=pltpu.CompilerParams(
            dimension_semantics=("arbitrary", "arbitrary"), vmem_limit_bytes=VMEM_LIMIT),
        name="post_mlp",
    )(x, p, ypool, oT, wo, wo, g_mix_post, g_mlp_pre, g_mlp_post, wup, wdown, wgate, wproj)
    return out


def kernel(x, p, w_in, w_pool, pool_scale, w_out, rel_bias, g_mix_pre, g_mix_post,
           g_mlp_pre, g_mlp_post, w_up, w_down, w_ple_proj, w_ple_gate):
    depth = w_in.shape[0]
    h = x
    for i in range(depth):
        h = _layer(h, p[i], w_in[i], w_pool[i], pool_scale[i:i + 1], w_out[i], rel_bias,
                   g_mix_pre[i:i + 1], g_mix_post[i:i + 1], g_mlp_pre[i:i + 1],
                   g_mlp_post[i:i + 1], w_up[i], w_down[i], w_ple_proj[i], w_ple_gate[i])
    return h
```

```python
import functools
import math

import numpy as np
import jax
import jax.numpy as jnp
from jax import lax
from jax.experimental import pallas as pl
from jax.experimental.pallas import tpu as pltpu

D_MODEL = 1024
POOL_WIDTH = 512
POOL_WINDOWS = (2, 4, 8, 16)
POOL_CH = 128
POOL_HALO = 16
ATTN_WIDTH = 512
HEAD_DIM = 64
HEADS = 8
BLOCK = 256
TOP_K = 3
NUM_BUCKETS = 32
MAX_DISTANCE = 1024
D_FF = 4096
PLE_DIM = 256
EPS = 1e-6
NEG = -1e30
LOG2E = math.log2(math.e)
KAUG = 128
NEAR = 5
CHUNK = 4
CHUNK_KEYS = CHUNK * BLOCK
ACC_ROWS = HEAD_DIM + 16
SELECT_UNROLL = 16
LAG_OWN = 8
LAG_PAST = 16
LAG_MAX = max(LAG_OWN, LAG_PAST)

TM_IN = 1024
TM_POST = 512
POST_PARTS = 2
VMEM_LIMIT = 56 * 1024 * 1024


def _bucket_thresholds():
    n = np.arange(0, NEAR * BLOCK + BLOCK, dtype=np.int64)
    max_exact = NUM_BUCKETS // 2
    nf = np.maximum(n, 1).astype(np.float64)
    large = max_exact + (np.log(nf / max_exact) / math.log(MAX_DISTANCE / max_exact)
                         * (NUM_BUCKETS - max_exact)).astype(np.int64)
    large = np.minimum(large, NUM_BUCKETS - 1)
    bucket = np.where(n < max_exact, n, large)
    assert np.all(np.diff(bucket) >= 0)
    return [int(np.argmax(bucket >= b)) for b in range(NUM_BUCKETS)], bucket


_THRESH, _BUCKET = _bucket_thresholds()
assert _BUCKET[(NEAR - 1) * BLOCK + 1] == NUM_BUCKETS - 1
assert NEAR <= CHUNK + 1
assert POOL_WIDTH == ATTN_WIDTH


def _rms(x, g):
    ms = jnp.mean(x * x, axis=-1, keepdims=True)
    return x * lax.rsqrt(ms + EPS) * g


def _inproj_kernel(n_cast, x_ref, g_ref, win_ref, wpool_ref, pscale_ref, *refs):
    cast_in, refs = refs[:n_cast], refs[n_cast:]
    ypool_ref, qT_ref, kaug_ref, kbar_ref, vT_ref = refs[:5]
    cast_out, (halo_scr, win_scr, wpool_scr) = refs[5:5 + n_cast], refs[5 + n_cast:]
    t = pl.program_id(1)

    @pl.when((pl.program_id(0) == 0) & (t == 0))
    def _():
        win_scr[...] = win_ref[...].astype(win_scr.dtype)
        wpool_scr[...] = wpool_ref[...].astype(wpool_scr.dtype)

    def w_cols(start, width):
        return win_scr[:, start:start + width]

    for src, dst in zip(cast_in, cast_out):
        dst[...] = src[...].astype(dst.dtype)
    tm = x_ref.shape[1]
    nblk = tm // BLOCK

    @pl.when(t == 0)
    def _():
        halo_scr[...] = jnp.zeros(halo_scr.shape, jnp.float32)

    a = _rms(x_ref[0], g_ref[...]).astype(jnp.bfloat16)

    zk = jnp.dot(a, w_cols(POOL_WIDTH + ATTN_WIDTH, ATTN_WIDTH),
                 preferred_element_type=jnp.float32)
    lane = lax.broadcasted_iota(jnp.int32, (tm, KAUG), 1)
    gblk = t * nblk + lax.broadcasted_iota(jnp.int32, (tm, KAUG), 0) // BLOCK
    onehot = jnp.where((lane == HEAD_DIM + gblk) | (lane == HEAD_DIM + 32 + gblk), 1.0, 0.0)
    for h in range(HEADS):
        pair = zk[:, (h // 2) * KAUG:(h // 2 + 1) * KAUG]
        if h % 2:
            pair = pltpu.roll(pair, HEAD_DIM, axis=1)
        kaug_ref[0, h] = jnp.where(lane < HEAD_DIM, pair, onehot).astype(kaug_ref.dtype)
    for bl in range(nblk):
        kbar_ref[0, bl] = jnp.mean(zk[bl * BLOCK:(bl + 1) * BLOCK, :], axis=0, keepdims=True)

    u = jnp.dot(a, w_cols(0, POOL_WIDTH), preferred_element_type=jnp.float32)

    zq = jnp.dot(a, w_cols(POOL_WIDTH, ATTN_WIDTH), preferred_element_type=jnp.float32)
    for bl in range(nblk):
        qT_ref[0, bl] = zq[bl * BLOCK:(bl + 1) * BLOCK, :].T

    tpos = t * tm + lax.broadcasted_iota(jnp.int32, (tm, 1), 0)
    for g, w in enumerate(POOL_WINDOWS):
        cs = slice(g * POOL_CH, (g + 1) * POOL_CH)
        ug = u[:, cs]
        acc = jnp.concatenate([halo_scr[:, cs], ug], axis=0)
        span = 1
        while span < w:
            acc = acc + pltpu.roll(acc, span, axis=0)
            span *= 2
        cnt = jnp.minimum(tpos + 1, w).astype(jnp.float32)
        d = acc[POOL_HALO:] / cnt - ug
        y = jnp.dot(d.astype(jnp.bfloat16), wpool_scr[g], preferred_element_type=jnp.float32)
        ypool_ref[0, :, cs] = (y * pscale_ref[:, cs]).astype(ypool_ref.dtype)
    halo_scr[...] = u[tm - POOL_HALO:tm, :]

    zv = jnp.dot(a, w_cols(POOL_WIDTH + 2 * ATTN_WIDTH, ATTN_WIDTH),
                 preferred_element_type=jnp.float32)
    for bl in range(nblk):
        vT_ref[0, bl] = zv[bl * BLOCK:(bl + 1) * BLOCK, :].T.astype(vT_ref.dtype)


def _bias_tile(tab_ref, h, d):
    r = lax.broadcasted_iota(jnp.int32, (BLOCK, BLOCK), 0)
    c = lax.broadcasted_iota(jnp.int32, (BLOCK, BLOCK), 1)
    n = d * BLOCK + c - r
    lo, hi = (0 if d == 0 else (d - 1) * BLOCK + 1), d * BLOCK + BLOCK - 1
    b0 = int(_BUCKET[lo])
    tile = jnp.full((BLOCK, BLOCK), tab_ref[b0, h] * LOG2E, jnp.float32)
    for b in range(b0 + 1, NUM_BUCKETS):
        if _THRESH[b] > hi:
            break
        tile = jnp.where(n >= _THRESH[b], tab_ref[b, h] * LOG2E, tile)
    if d == 0:
        tile = jnp.where(n >= 0, tile, NEG)
    return tile


def _fold8(x):
    return x.reshape(x.shape[0] // 8, 8, x.shape[1])


def _attn_items(nb):
    own = [(i, i // CHUNK) for i in range(nb)]
    past = [(i, c) for i in range(nb) for c in range(i // CHUNK)]
    return tuple(np.asarray(v, np.int32).T.copy() for v in (own, past))


def _attn_kernel(tab_ref, own_ref, past_ref, qT_ref, kaug_ref, vT_ref, kbar_ref, oT_ref,
                 qaug_scr, s_buf, p_buf, m_st, acc_st, bias_scr):
    h = pl.program_id(0)
    b = pl.program_id(1)
    nb = kbar_ref.shape[1]

    @pl.when(b == 0)
    def _():
        for d in range(bias_scr.shape[0]):
            if d < NEAR:
                bias_scr[d] = _bias_tile(tab_ref, h, d)
            else:
                bias_scr[d] = jnp.zeros((BLOCK, BLOCK), jnp.float32)

    kb2 = kbar_ref[0, :, 0, :]
    kb = jnp.where(h % 2 == 0, kb2[:, :HEAD_DIM], kb2[:, HEAD_DIM:])
    kb_hi = kb.astype(jnp.bfloat16)
    kb_lo = (kb - kb_hi.astype(jnp.float32)).astype(jnp.bfloat16)
    jidx = lax.broadcasted_iota(jnp.int32, (nb, BLOCK), 0)
    cfar = jnp.full((nb, BLOCK), tab_ref[NUM_BUCKETS - 1, h] * LOG2E, jnp.float32)
    cfar_hi = cfar.astype(jnp.bfloat16).astype(jnp.float32)
    cfar_lo = cfar - cfar_hi

    def select(i):
        qT = qT_ref[0, i]
        q_hi = qT.astype(jnp.bfloat16)
        q_lo = (qT - q_hi.astype(jnp.float32)).astype(jnp.bfloat16)
        sc = (jnp.dot(kb_hi, q_hi, preferred_element_type=jnp.float32)
              + (jnp.dot(kb_hi, q_lo, preferred_element_type=jnp.float32)
                 + jnp.dot(kb_lo, q_hi, preferred_element_type=jnp.float32)))
        past = jidx < i
        work = jnp.where(past, sc, NEG)
        chosen = jnp.zeros((nb, BLOCK), jnp.bool_)
        for _ in range(TOP_K):
            mx = jnp.max(work, axis=0, keepdims=True)
            first = jnp.min(jnp.where(work == mx, jidx, nb), axis=0, keepdims=True)
            pick = jidx == first
            chosen = chosen | pick
            work = jnp.where(pick, -jnp.inf, work)
        sel = chosen & past
        far = jidx <= i - NEAR
        code_hi = jnp.where(past, jnp.where(sel, jnp.where(far, cfar_hi, 0.0), NEG),
                            jnp.where(jidx == i, 0.0, NEG))
        code_lo = jnp.where(sel & far, cfar_lo, 0.0)
        qaug_scr[i] = jnp.concatenate(
            [(qT * (HEAD_DIM ** -0.5 * LOG2E)).astype(jnp.bfloat16),
             code_hi.astype(jnp.bfloat16), code_lo.astype(jnp.bfloat16)], axis=0)

    def select_group(g, carry):
        for u in range(SELECT_UNROLL):
            select(g * SELECT_UNROLL + u)
        return carry

    n_groups = nb // SELECT_UNROLL
    lax.fori_loop(0, n_groups - 1, select_group, 0)

    def stage_qk(item, slot, near, nblk):
        i, c = item
        q_aug = qaug_scr[i]
        cm = None
        for w in range(nblk):
            row0 = pl.multiple_of((c * CHUNK + w) * BLOCK, BLOCK)
            sw = jnp.dot(kaug_ref[0, 0, pl.ds(row0, BLOCK), :], q_aug,
                         preferred_element_type=jnp.float32)
            if near:
                sw = sw + bias_scr[jnp.minimum(i - (c * CHUNK + w), bias_scr.shape[0] - 1)]
            s_buf[slot, w] = sw
            fm = jnp.max(_fold8(sw), axis=0)
            cm = fm if cm is None else jnp.maximum(cm, fm)
        return cm

    def stage_exp(item, slot, cm, nblk, first):
        i, _ = item
        col_max = jnp.max(cm, axis=0, keepdims=True)
        if first:
            m_new = jnp.broadcast_to(col_max, cm.shape)
            alpha = None
        else:
            m_old = m_st[i]
            m_new = jnp.maximum(m_old, col_max)
            alpha = jnp.exp2(m_old - m_new)[0:1]
        m_st[i] = m_new
        m_row = m_new[0:1]
        for w in range(nblk):
            p = jnp.exp2(s_buf[slot, w] - m_row)
            p_buf[slot, w * BLOCK:(w + 1) * BLOCK, :] = p.astype(jnp.bfloat16)
        return alpha

    def ones_rows(keys):
        first = lax.broadcasted_iota(jnp.int32, (ACC_ROWS - HEAD_DIM, keys), 0) == 0
        return first.astype(jnp.bfloat16)

    def stage_pv(item, slot, alpha, nblk):
        i, c = item
        vblk = jnp.concatenate([vT_ref[0, c * CHUNK + w] for w in range(nblk)], axis=1)
        vblk = jnp.concatenate([vblk, ones_rows(nblk * BLOCK)], axis=0)
        pv = jnp.dot(vblk, p_buf[slot, :nblk * BLOCK, :],
                     preferred_element_type=jnp.float32)
        acc_st[i] = pv if alpha is None else alpha * acc_st[i] + pv

    def run(items_ref, near, lag, nblk_of_slot=lambda j: CHUNK, first=False,
            after_fill=lambda: None):
        n = items_ref.shape[1]
        assert n % lag == 0 and n >= 2 * lag and lag <= s_buf.shape[0]
        item = lambda x: (items_ref[0, x], items_ref[1, x])

        def softmax_pv(x, j, cm):
            alpha = stage_exp(item(x), j, cm, nblk_of_slot(j), first)
            stage_pv(item(x), j, alpha, nblk_of_slot(j))

        cms = tuple(stage_qk(item(x), x, near, nblk_of_slot(x)) for x in range(lag))
        after_fill()

        def body(k, cms):
            t = lag * k
            new = []
            for j in range(lag):
                softmax_pv(t - lag + j, j, cms[j])
                new.append(stage_qk(item(t + j), j, near, nblk_of_slot(j)))
            return tuple(new)

        cms = lax.fori_loop(1, n // lag, body, cms)
        for j in range(lag):
            softmax_pv(n - lag + j, j, cms[j])

    assert LAG_OWN <= (n_groups - 1) * SELECT_UNROLL
    run(own_ref, True, LAG_OWN, lambda j: j % CHUNK + 1, first=True,
        after_fill=lambda: select_group(n_groups - 1, 0))
    run(past_ref, True, LAG_PAST)

    def finish(i, carry):
        acc = acc_st[i]
        oT_ref[0, i] = (acc[:HEAD_DIM] / acc[HEAD_DIM:HEAD_DIM + 1]).astype(oT_ref.dtype)
        return carry

    lax.fori_loop(0, nb, finish, 0)


def _post_kernel(x_ref, p_ref, ypool_ref, oT_ref, wop_ref, woa_ref, gpost_ref, gpre_ref,
                 gmpost_ref, wup_ref, wdown_ref, wgate_ref, wproj_ref, out_ref):
    nblk = oT_ref.shape[1]
    yat = jnp.concatenate([oT_ref[0, bl].T for bl in range(nblk)], axis=0)
    mix = (jnp.dot(ypool_ref[0], wop_ref[...], preferred_element_type=jnp.float32)
           + jnp.dot(yat.astype(jnp.bfloat16), woa_ref[...], preferred_element_type=jnp.float32))
    pe = jnp.dot(p_ref[0].astype(jnp.bfloat16), wproj_ref[...], preferred_element_type=jnp.float32)
    h1 = x_ref[0] + _rms(mix, gpost_ref[...])
    m = _rms(h1, gpre_ref[...]).astype(jnp.bfloat16)
    up = jnp.dot(m, wup_ref[...], preferred_element_type=jnp.float32)
    act = jnp.square(jnp.maximum(up, 0.0)).astype(jnp.bfloat16)
    part = act.shape[0] // POST_PARTS
    parts = [slice(k * part, (k + 1) * part) for k in range(POST_PARTS)]
    fs = [jnp.dot(act[rows], wdown_ref[...], preferred_element_type=jnp.float32) for rows in parts]
    for rows, f in zip(parts, fs):
        h2 = h1[rows] + _rms(f, gmpost_ref[...])
        gate = jax.nn.sigmoid(jnp.dot(h2.astype(jnp.bfloat16), wgate_ref[...],
                                      preferred_element_type=jnp.float32))
        out_ref[0, rows, :] = h2 + gate * pe[rows]


def _resident(shape):
    return pl.BlockSpec(shape, lambda *_: (0,) * len(shape), pipeline_mode=pl.Buffered(1))


def _mixers(x, w_in, w_pool, pool_scale, rel_bias, g_mix_pre, later_weights=()):
    B, S, _ = x.shape
    nb = S // BLOCK
    bf16 = jnp.bfloat16
    f32 = jnp.float32

    tm = TM_IN
    nblk = tm // BLOCK
    steps_b = S // tm
    slab = lambda w: pl.BlockSpec((w.shape[0] // (B * steps_b), w.shape[1]),
                                  lambda b, t: (b * steps_b + t, 0))
    ypool, qT, kaug, kbar, vT, *cast = pl.pallas_call(
        functools.partial(_inproj_kernel, len(later_weights)),
        grid=(B, steps_b),
        in_specs=[
            pl.BlockSpec((1, tm, D_MODEL), lambda b, t: (b, t, 0)),
            _resident((1, D_MODEL)),
            _resident(w_in.shape),
            _resident(w_pool.shape),
            _resident((1, POOL_WIDTH)),
            *[slab(w) for w in later_weights],
        ],
        out_specs=[
            pl.BlockSpec((1, tm, POOL_WIDTH), lambda b, t: (b, t, 0)),
            pl.BlockSpec((1, nblk, ATTN_WIDTH, BLOCK), lambda b, t: (b, t, 0, 0)),
            pl.BlockSpec((1, HEADS, tm, KAUG), lambda b, t: (b, 0, t, 0)),
            pl.BlockSpec((1, nblk, 1, ATTN_WIDTH), lambda b, t: (b, t, 0, 0)),
            pl.BlockSpec((1, nblk, ATTN_WIDTH, BLOCK), lambda b, t: (b, t, 0, 0)),
            *[slab(w) for w in later_weights],
        ],
        out_shape=[
            jax.ShapeDtypeStruct((B, S, POOL_WIDTH), bf16),
            jax.ShapeDtypeStruct((B, nb, ATTN_WIDTH, BLOCK), f32),
            jax.ShapeDtypeStruct((B, HEADS, S, KAUG), bf16),
            jax.ShapeDtypeStruct((B, nb, 1, ATTN_WIDTH), f32),
            jax.ShapeDtypeStruct((B, nb, ATTN_WIDTH, BLOCK), bf16),
            *[jax.ShapeDtypeStruct(w.shape, bf16) for w in later_weights],
        ],
        scratch_shapes=[pltpu.VMEM((POOL_HALO, POOL_WIDTH), f32),
                        pltpu.VMEM(w_in.shape, bf16), pltpu.VMEM(w_pool.shape, bf16)],
        compiler_params=pltpu.CompilerParams(
            dimension_semantics=("arbitrary", "arbitrary"), vmem_limit_bytes=VMEM_LIMIT),
        name="inproj_pool",
    )(x, g_mix_pre, w_in, w_pool, pool_scale, *later_weights)

    own_items, past_items = _attn_items(nb)
    smem = pl.BlockSpec(memory_space=pltpu.SMEM)
    oT = pl.pallas_call(
        _attn_kernel,
        grid=(HEADS, B),
        in_specs=[
            smem, smem, smem,
            pl.BlockSpec((1, nb, HEAD_DIM, BLOCK), lambda h, b: (b, 0, h, 0)),
            pl.BlockSpec((1, 1, S, KAUG), lambda h, b: (b, h, 0, 0)),
            pl.BlockSpec((1, nb, HEAD_DIM, BLOCK), lambda h, b: (b, 0, h, 0)),
            pl.BlockSpec((1, nb, 1, 2 * HEAD_DIM), lambda h, b: (b, 0, 0, h // 2)),
        ],
        out_specs=pl.BlockSpec((1, nb, HEAD_DIM, BLOCK), lambda h, b: (b, 0, h, 0)),
        out_shape=jax.ShapeDtypeStruct((B, nb, ATTN_WIDTH, BLOCK), f32),
        scratch_shapes=[
            pltpu.VMEM((nb, KAUG, BLOCK), bf16),
            pltpu.VMEM((LAG_MAX, CHUNK, BLOCK, BLOCK), f32),
            pltpu.VMEM((LAG_MAX, CHUNK_KEYS, BLOCK), bf16),
            pltpu.VMEM((nb, 8, BLOCK), f32),
            pltpu.VMEM((nb, ACC_ROWS, BLOCK), f32),
            pltpu.VMEM((2 * CHUNK, BLOCK, BLOCK), f32),
        ],
        compiler_params=pltpu.CompilerParams(
            dimension_semantics=("arbitrary", "arbitrary"), vmem_limit_bytes=VMEM_LIMIT),
        name="moba_attention",
    )(rel_bias, jnp.asarray(own_items), jnp.asarray(past_items), qT, kaug, vT, kbar)
    return ypool, oT, cast


def _layer(x, p, w_in, w_pool, pool_scale, w_out, rel_bias, g_mix_pre, g_mix_post,
           g_mlp_pre, g_mlp_post, w_up, w_down, w_ple_proj, w_ple_gate):
    B, S, _ = x.shape
    bf16 = jnp.bfloat16
    f32 = jnp.float32
    ypool, oT, (wo, wup, wdown, wgate, wproj) = _mixers(
        x, w_in, w_pool, pool_scale, rel_bias, g_mix_pre,
        later_weights=(w_out, w_up, w_down, w_ple_gate, w_ple_proj))

    tm = TM_POST
    nblk = tm // BLOCK
    out = pl.pallas_call(
        _post_kernel,
        grid=(B, S // tm),
        in_specs=[
            pl.BlockSpec((1, tm, D_MODEL), lambda b, t: (b, t, 0)),
            pl.BlockSpec((1, tm, PLE_DIM), lambda b, t: (b, t, 0)),
            pl.BlockSpec((1, tm, POOL_WIDTH), lambda b, t: (b, t, 0)),
            pl.BlockSpec((1, nblk, ATTN_WIDTH, BLOCK), lambda b, t: (b, t, 0, 0)),
            pl.BlockSpec((POOL_WIDTH, D_MODEL), lambda b, t: (0, 0), pipeline_mode=pl.Buffered(1)),
            pl.BlockSpec((ATTN_WIDTH, D_MODEL), lambda b, t: (1, 0), pipeline_mode=pl.Buffered(1)),
            _resident((1, D_MODEL)),
            _resident((1, D_MODEL)),
            _resident((1, D_MODEL)),
            _resident((D_MODEL, D_FF)),
            _resident((D_FF, D_MODEL)),
            _resident((D_MODEL, D_MODEL)),
            _resident((PLE_DIM, D_MODEL)),
        ],
        out_specs=pl.BlockSpec((1, tm, D_MODEL), lambda b, t: (b, t, 0)),
        out_shape=jax.ShapeDtypeStruct((B, S, D_MODEL), f32),
        compiler_params=pltpu.CompilerParams(
            dimension_semantics=("arbitrary", "arbitrary"), vmem_limit_bytes=VMEM_LIMIT),
        name="post_mlp",
    )(x, p, ypool, oT, wo, wo, g_mix_post, g_mlp_pre, g_mlp_post, wup, wdown, wgate, wproj)
    return out


def kernel(x, p, w_in, w_pool, pool_scale, w_out, rel_bias, g_mix_pre, g_mix_post,
           g_mlp_pre, g_mlp_post, w_up, w_down, w_ple_proj, w_ple_gate):
    depth = w_in.shape[0]
    h = x
    for i in range(depth):
        h = _layer(h, p[i], w_in[i], w_pool[i], pool_scale[i:i + 1], w_out[i], rel_bias,
                   g_mix_pre[i:i + 1], g_mix_post[i:i + 1], g_mlp_pre[i:i + 1],
                   g_mlp_post[i:i + 1], w_up[i], w_down[i], w_ple_proj[i], w_ple_gate[i])
    return h
```

```python
import functools
import math

import numpy as np
import jax
import jax.numpy as jnp
from jax import lax
from jax.experimental import pallas as pl
from jax.experimental.pallas import tpu as pltpu

D_MODEL = 1024
POOL_WIDTH = 512
POOL_WINDOWS = (2, 4, 8, 16)
POOL_CH = 128
POOL_HALO = 16
ATTN_WIDTH = 512
HEAD_DIM = 64
HEADS = 8
BLOCK = 256
TOP_K = 3
NUM_BUCKETS = 32
MAX_DISTANCE = 1024
D_FF = 4096
PLE_DIM = 256
EPS = 1e-6
NEG = -1e30
LOG2E = math.log2(math.e)
KAUG = 128
NEAR = 5
CHUNK = 4
CHUNK_KEYS = CHUNK * BLOCK
ACC_ROWS = HEAD_DIM + 16
SELECT_UNROLL = 16
LAG_OWN = 8
LAG_PAST = 16
LAG_MAX = max(LAG_OWN, LAG_PAST)

TM_IN = 1024
TM_POST = 512
POST_PARTS = 2
VMEM_LIMIT = 56 * 1024 * 1024


def _bucket_thresholds():
    n = np.arange(0, NEAR * BLOCK + BLOCK, dtype=np.int64)
    max_exact = NUM_BUCKETS // 2
    nf = np.maximum(n, 1).astype(np.float64)
    large = max_exact + (np.log(nf / max_exact) / math.log(MAX_DISTANCE / max_exact)
                         * (NUM_BUCKETS - max_exact)).astype(np.int64)
    large = np.minimum(large, NUM_BUCKETS - 1)
    bucket = np.where(n < max_exact, n, large)
    assert np.all(np.diff(bucket) >= 0)
    return [int(np.argmax(bucket >= b)) for b in range(NUM_BUCKETS)], bucket


_THRESH, _BUCKET = _bucket_thresholds()
assert _BUCKET[(NEAR - 1) * BLOCK + 1] == NUM_BUCKETS - 1
assert NEAR <= CHUNK + 1
assert POOL_WIDTH == ATTN_WIDTH


def _rms(x, g):
    ms = jnp.mean(x * x, axis=-1, keepdims=True)
    return x * lax.rsqrt(ms + EPS) * g


def _inproj_kernel(n_cast, x_ref, g_ref, win_ref, wpool_ref, pscale_ref, *refs):
    cast_in, refs = refs[:n_cast], refs[n_cast:]
    ypool_ref, qT_ref, kaug_ref, kbar_ref, vT_ref = refs[:5]
    cast_out, (halo_scr, win_scr, wpool_scr) = refs[5:5 + n_cast], refs[5 + n_cast:]
    t = pl.program_id(1)

    @pl.when((pl.program_id(0) == 0) & (t == 0))
    def _():
        win_scr[...] = win_ref[...].astype(win_scr.dtype)
        wpool_scr[...] = wpool_ref[...].astype(wpool_scr.dtype)

    def w_cols(start, width):
        return win_scr[:, start:start + width]

    for src, dst in zip(cast_in, cast_out):
        dst[...] = src[...].astype(dst.dtype)
    tm = x_ref.shape[1]
    nblk = tm // BLOCK

    @pl.when(t == 0)
    def _():
        halo_scr[...] = jnp.zeros(halo_scr.shape, jnp.float32)

    a = _rms(x_ref[0], g_ref[...]).astype(jnp.bfloat16)

    zk = jnp.dot(a, w_cols(POOL_WIDTH + ATTN_WIDTH, ATTN_WIDTH),
                 preferred_element_type=jnp.float32)
    lane = lax.broadcasted_iota(jnp.int32, (tm, KAUG), 1)
    gblk = t * nblk + lax.broadcasted_iota(jnp.int32, (tm, KAUG), 0) // BLOCK
    onehot = jnp.where((lane == HEAD_DIM + gblk) | (lane == HEAD_DIM + 32 + gblk), 1.0, 0.0)
    for h in range(HEADS):
        pair = zk[:, (h // 2) * KAUG:(h // 2 + 1) * KAUG]
        if h % 2:
            pair = pltpu.roll(pair, HEAD_DIM, axis=1)
        kaug_ref[0, h] = jnp.where(lane < HEAD_DIM, pair, onehot).astype(kaug_ref.dtype)
    for bl in range(nblk):
        kbar_ref[0, bl] = jnp.mean(zk[bl * BLOCK:(bl + 1) * BLOCK, :], axis=0, keepdims=True)

    u = jnp.dot(a, w_cols(0, POOL_WIDTH), preferred_element_type=jnp.float32)

    zq = jnp.dot(a, w_cols(POOL_WIDTH, ATTN_WIDTH), preferred_element_type=jnp.float32)
    for bl in range(nblk):
        qT_ref[0, bl] = zq[bl * BLOCK:(bl + 1) * BLOCK, :].T

    tpos = t * tm + lax.broadcasted_iota(jnp.int32, (tm, 1), 0)
    for g, w in enumerate(POOL_WINDOWS):
        cs = slice(g * POOL_CH, (g + 1) * POOL_CH)
        ug = u[:, cs]
        acc = jnp.concatenate([halo_scr[:, cs], ug], axis=0)
        span = 1
        while span < w:
            acc = acc + pltpu.roll(acc, span, axis=0)
            span *= 2
        cnt = jnp.minimum(tpos + 1, w).astype(jnp.float32)
        d = acc[POOL_HALO:] / cnt - ug
        y = jnp.dot(d.astype(jnp.bfloat16), wpool_scr[g], preferred_element_type=jnp.float32)
        ypool_ref[0, :, cs] = (y * pscale_ref[:, cs]).astype(ypool_ref.dtype)
    halo_scr[...] = u[tm - POOL_HALO:tm, :]

    zv = jnp.dot(a, w_cols(POOL_WIDTH + 2 * ATTN_WIDTH, ATTN_WIDTH),
                 preferred_element_type=jnp.float32)
    for bl in range(nblk):
        vT_ref[0, bl] = zv[bl * BLOCK:(bl + 1) * BLOCK, :].T.astype(vT_ref.dtype)


def _bias_tile(tab_ref, h, d):
    r = lax.broadcasted_iota(jnp.int32, (BLOCK, BLOCK), 0)
    c = lax.broadcasted_iota(jnp.int32, (BLOCK, BLOCK), 1)
    n = d * BLOCK + c - r
    lo, hi = (0 if d == 0 else (d - 1) * BLOCK + 1), d * BLOCK + BLOCK - 1
    b0 = int(_BUCKET[lo])
    tile = jnp.full((BLOCK, BLOCK), tab_ref[b0, h] * LOG2E, jnp.float32)
    for b in range(b0 + 1, NUM_BUCKETS):
        if _THRESH[b] > hi:
            break
        tile = jnp.where(n >= _THRESH[b], tab_ref[b, h] * LOG2E, tile)
    if d == 0:
        tile = jnp.where(n >= 0, tile, NEG)
    return tile


def _fold8(x):
    return x.reshape(x.shape[0] // 8, 8, x.shape[1])


def _attn_items(nb):
    own = [(i, i // CHUNK) for i in range(nb)]
    past = [(i, c) for i in range(nb) for c in range(i // CHUNK)]
    return tuple(np.asarray(v, np.int32).T.copy() for v in (own, past))


def _attn_kernel(tab_ref, own_ref, past_ref, qT_ref, kaug_ref, vT_ref, kbar_ref, oT_ref,
                 qaug_scr, s_buf, p_buf, m_st, acc_st, bias_scr):
    h = pl.program_id(0)
    b = pl.program_id(1)
    nb = kbar_ref.shape[1]

    @pl.when(b == 0)
    def _():
        for d in range(bias_scr.shape[0]):
            if d < NEAR:
                bias_scr[d] = _bias_tile(tab_ref, h, d)
            else:
                bias_scr[d] = jnp.zeros((BLOCK, BLOCK), jnp.float32)

    kb2 = kbar_ref[0, :, 0, :]
    kb = jnp.where(h % 2 == 0, kb2[:, :HEAD_DIM], kb2[:, HEAD_DIM:])
    kb_hi = kb.astype(jnp.bfloat16)
    kb_lo = (kb - kb_hi.astype(jnp.float32)).astype(jnp.bfloat16)
    jidx = lax.broadcasted_iota(jnp.int32, (nb, BLOCK), 0)
    cfar = jnp.full((nb, BLOCK), tab_ref[NUM_BUCKETS - 1, h] * LOG2E, jnp.float32)
    cfar_hi = cfar.astype(jnp.bfloat16).astype(jnp.float32)
    cfar_lo = cfar - cfar_hi

    def select(i):
        qT = qT_ref[0, i]
        q_hi = qT.astype(jnp.bfloat16)
        q_lo = (qT - q_hi.astype(jnp.float32)).astype(jnp.bfloat16)
        sc = (jnp.dot(kb_hi, q_hi, preferred_element_type=jnp.float32)
              + (jnp.dot(kb_hi, q_lo, preferred_element_type=jnp.float32)
                 + jnp.dot(kb_lo, q_hi, preferred_element_type=jnp.float32)))
        past = jidx < i
        work = jnp.where(past, sc, NEG)
        chosen = jnp.zeros((nb, BLOCK), jnp.bool_)
        for _ in range(TOP_K):
            mx = jnp.max(work, axis=0, keepdims=True)
            first = jnp.min(jnp.where(work == mx, jidx, nb), axis=0, keepdims=True)
            pick = jidx == first
            chosen = chosen | pick
            work = jnp.where(pick, -jnp.inf, work)
        sel = chosen & past
        far = jidx <= i - NEAR
        code_hi = jnp.where(past, jnp.where(sel, jnp.where(far, cfar_hi, 0.0), NEG),
                            jnp.where(jidx == i, 0.0, NEG))
        code_lo = jnp.where(sel & far, cfar_lo, 0.0)
        qaug_scr[i] = jnp.concatenate(
            [(qT * (HEAD_DIM ** -0.5 * LOG2E)).astype(jnp.bfloat16),
             code_hi.astype(jnp.bfloat16), code_lo.astype(jnp.bfloat16)], axis=0)

    def select_group(g, carry):
        for u in range(SELECT_UNROLL):
            select(g * SELECT_UNROLL + u)
        return carry

    n_groups = nb // SELECT_UNROLL
    lax.fori_loop(0, n_groups - 1, select_group, 0)

    def stage_qk(item, slot, near, nblk):
        i, c = item
        q_aug = qaug_scr[i]
        cm = None
        for w in range(nblk):
            row0 = pl.multiple_of((c * CHUNK + w) * BLOCK, BLOCK)
            sw = jnp.dot(kaug_ref[0, 0, pl.ds(row0, BLOCK), :], q_aug,
                         preferred_element_type=jnp.float32)
            if near:
                sw = sw + bias_scr[jnp.minimum(i - (c * CHUNK + w), bias_scr.shape[0] - 1)]
            s_buf[slot, w] = sw
            fm = jnp.max(_fold8(sw), axis=0)
            cm = fm if cm is None else jnp.maximum(cm, fm)
        return cm

    def stage_exp(item, slot, cm, nblk, first):
        i, _ = item
        col_max = jnp.max(cm, axis=0, keepdims=True)
        if first:
            m_new = jnp.broadcast_to(col_max, cm.shape)
            alpha = None
        else:
            m_old = m_st[i]
            m_new = jnp.maximum(m_old, col_max)
            alpha = jnp.exp2(m_old - m_new)[0:1]
        m_st[i] = m_new
        m_row = m_new[0:1]
        for w in range(nblk):
            p = jnp.exp2(s_buf[slot, w] - m_row)
            p_buf[slot, w * BLOCK:(w + 1) * BLOCK, :] = p.astype(jnp.bfloat16)
        return alpha

    def ones_rows(keys):
        first = lax.broadcasted_iota(jnp.int32, (ACC_ROWS - HEAD_DIM, keys), 0) == 0
        return first.astype(jnp.bfloat16)

    def stage_pv(item, slot, alpha, nblk):
        i, c = item
        vblk = jnp.concatenate([vT_ref[0, c * CHUNK + w] for w in range(nblk)], axis=1)
        vblk = jnp.concatenate([vblk, ones_rows(nblk * BLOCK)], axis=0)
        pv = jnp.dot(vblk, p_buf[slot, :nblk * BLOCK, :],
                     preferred_element_type=jnp.float32)
        acc_st[i] = pv if alpha is None else alpha * acc_st[i] + pv

    def run(items_ref, near, lag, nblk_of_slot=lambda j: CHUNK, first=False,
            after_fill=lambda: None):
        n = items_ref.shape[1]
        assert n % lag == 0 and n >= 2 * lag and lag <= s_buf.shape[0]
        item = lambda x: (items_ref[0, x], items_ref[1, x])

        def softmax_pv(x, j, cm):
            alpha = stage_exp(item(x), j, cm, nblk_of_slot(j), first)
            stage_pv(item(x), j, alpha, nblk_of_slot(j))

        cms = tuple(stage_qk(item(x), x, near, nblk_of_slot(x)) for x in range(lag))
        after_fill()

        def body(k, cms):
            t = lag * k
            new = []
            for j in range(lag):
                softmax_pv(t - lag + j, j, cms[j])
                new.append(stage_qk(item(t + j), j, near, nblk_of_slot(j)))
            return tuple(new)

        cms = lax.fori_loop(1, n // lag, body, cms)
        for j in range(lag):
            softmax_pv(n - lag + j, j, cms[j])

    assert LAG_OWN <= (n_groups - 1) * SELECT_UNROLL
    run(own_ref, True, LAG_OWN, lambda j: j % CHUNK + 1, first=True,
        after_fill=lambda: select_group(n_groups - 1, 0))
    run(past_ref, True, LAG_PAST)

    def finish(i, carry):
        acc = acc_st[i]
        oT_ref[0, i] = (acc[:HEAD_DIM] / acc[HEAD_DIM:HEAD_DIM + 1]).astype(oT_ref.dtype)
        return carry

    lax.fori_loop(0, nb, finish, 0)


def _post_kernel(x_ref, p_ref, ypool_ref, oT_ref, wop_ref, woa_ref, gpost_ref, gpre_ref,
                 gmpost_ref, wup_ref, wdown_ref, wgate_ref, wproj_ref, out_ref):
    nblk = oT_ref.shape[1]
    yat = jnp.concatenate([oT_ref[0, bl].T for bl in range(nblk)], axis=0)
    part = x_ref.shape[1] // POST_PARTS
    parts = [slice(k * part, (k + 1) * part) for k in range(POST_PARTS)]
    yat = yat.astype(jnp.bfloat16)
    mixes = [jnp.dot(ypool_ref[0, rows, :], wop_ref[...], preferred_element_type=jnp.float32)
             + jnp.dot(yat[rows], woa_ref[...], preferred_element_type=jnp.float32)
             for rows in parts]
    pe = jnp.dot(p_ref[0].astype(jnp.bfloat16), wproj_ref[...], preferred_element_type=jnp.float32)
    h1s, acts = [], []
    for rows, mix in zip(parts, mixes):
        h1 = x_ref[0, rows, :] + _rms(mix, gpost_ref[...])
        m = _rms(h1, gpre_ref[...]).astype(jnp.bfloat16)
        up = jnp.dot(m, wup_ref[...], preferred_element_type=jnp.float32)
        h1s.append(h1)
        acts.append(jnp.square(jnp.maximum(up, 0.0)).astype(jnp.bfloat16))
    fs = [jnp.dot(act, wdown_ref[...], preferred_element_type=jnp.float32) for act in acts]
    for rows, h1, f in zip(parts, h1s, fs):
        h2 = h1 + _rms(f, gmpost_ref[...])
        gate = jax.nn.sigmoid(jnp.dot(h2.astype(jnp.bfloat16), wgate_ref[...],
                                      preferred_element_type=jnp.float32))
        out_ref[0, rows, :] = h2 + gate * pe[rows]


def _resident(shape):
    return pl.BlockSpec(shape, lambda *_: (0,) * len(shape), pipeline_mode=pl.Buffered(1))


def _mixers(x, w_in, w_pool, pool_scale, rel_bias, g_mix_pre, later_weights=()):
    B, S, _ = x.shape
    nb = S // BLOCK
    bf16 = jnp.bfloat16
    f32 = jnp.float32

    tm = TM_IN
    nblk = tm // BLOCK
    steps_b = S // tm
    slab = lambda w: pl.BlockSpec((w.shape[0] // (B * steps_b), w.shape[1]),
                                  lambda b, t: (b * steps_b + t, 0))
    ypool, qT, kaug, kbar, vT, *cast = pl.pallas_call(
        functools.partial(_inproj_kernel, len(later_weights)),
        grid=(B, steps_b),
        in_specs=[
            pl.BlockSpec((1, tm, D_MODEL), lambda b, t: (b, t, 0)),
            _resident((1, D_MODEL)),
            _resident(w_in.shape),
            _resident(w_pool.shape),
            _resident((1, POOL_WIDTH)),
            *[slab(w) for w in later_weights],
        ],
        out_specs=[
            pl.BlockSpec((1, tm, POOL_WIDTH), lambda b, t: (b, t, 0)),
            pl.BlockSpec((1, nblk, ATTN_WIDTH, BLOCK), lambda b, t: (b, t, 0, 0)),
            pl.BlockSpec((1, HEADS, tm, KAUG), lambda b, t: (b, 0, t, 0)),
            pl.BlockSpec((1, nblk, 1, ATTN_WIDTH), lambda b, t: (b, t, 0, 0)),
            pl.BlockSpec((1, nblk, ATTN_WIDTH, BLOCK), lambda b, t: (b, t, 0, 0)),
            *[slab(w) for w in later_weights],
        ],
        out_shape=[
            jax.ShapeDtypeStruct((B, S, POOL_WIDTH), bf16),
            jax.ShapeDtypeStruct((B, nb, ATTN_WIDTH, BLOCK), f32),
            jax.ShapeDtypeStruct((B, HEADS, S, KAUG), bf16),
            jax.ShapeDtypeStruct((B, nb, 1, ATTN_WIDTH), f32),
            jax.ShapeDtypeStruct((B, nb, ATTN_WIDTH, BLOCK), bf16),
            *[jax.ShapeDtypeStruct(w.shape, bf16) for w in later_weights],
        ],
        scratch_shapes=[pltpu.VMEM((POOL_HALO, POOL_WIDTH), f32),
                        pltpu.VMEM(w_in.shape, bf16), pltpu.VMEM(w_pool.shape, bf16)],
        compiler_params=pltpu.CompilerParams(
            dimension_semantics=("arbitrary", "arbitrary"), vmem_limit_bytes=VMEM_LIMIT),
        name="inproj_pool",
    )(x, g_mix_pre, w_in, w_pool, pool_scale, *later_weights)

    own_items, past_items = _attn_items(nb)
    smem = pl.BlockSpec(memory_space=pltpu.SMEM)
    oT = pl.pallas_call(
        _attn_kernel,
        grid=(HEADS, B),
        in_specs=[
            smem, smem, smem,
            pl.BlockSpec((1, nb, HEAD_DIM, BLOCK), lambda h, b: (b, 0, h, 0)),
            pl.BlockSpec((1, 1, S, KAUG), lambda h, b: (b, h, 0, 0)),
            pl.BlockSpec((1, nb, HEAD_DIM, BLOCK), lambda h, b: (b, 0, h, 0)),
            pl.BlockSpec((1, nb, 1, 2 * HEAD_DIM), lambda h, b: (b, 0, 0, h // 2)),
        ],
        out_specs=pl.BlockSpec((1, nb, HEAD_DIM, BLOCK), lambda h, b: (b, 0, h, 0)),
        out_shape=jax.ShapeDtypeStruct((B, nb, ATTN_WIDTH, BLOCK), f32),
        scratch_shapes=[
            pltpu.VMEM((nb, KAUG, BLOCK), bf16),
            pltpu.VMEM((LAG_MAX, CHUNK, BLOCK, BLOCK), f32),
            pltpu.VMEM((LAG_MAX, CHUNK_KEYS, BLOCK), bf16),
            pltpu.VMEM((nb, 8, BLOCK), f32),
            pltpu.VMEM((nb, ACC_ROWS, BLOCK), f32),
            pltpu.VMEM((2 * CHUNK, BLOCK, BLOCK), f32),
        ],
        compiler_params=pltpu.CompilerParams(
            dimension_semantics=("arbitrary", "arbitrary"), vmem_limit_bytes=VMEM_LIMIT),
        name="moba_attention",
    )(rel_bias, jnp.asarray(own_items), jnp.asarray(past_items), qT, kaug, vT, kbar)
    return ypool, oT, cast


def _layer(x, p, w_in, w_pool, pool_scale, w_out, rel_bias, g_mix_pre, g_mix_post,
           g_mlp_pre, g_mlp_post, w_up, w_down, w_ple_proj, w_ple_gate):
    B, S, _ = x.shape
    bf16 = jnp.bfloat16
    f32 = jnp.float32
    ypool, oT, (wo, wup, wdown, wgate, wproj) = _mixers(
        x, w_in, w_pool, pool_scale, rel_bias, g_mix_pre,
        later_weights=(w_out, w_up, w_down, w_ple_gate, w_ple_proj))

    tm = TM_POST
    nblk = tm // BLOCK
    out = pl.pallas_call(
        _post_kernel,
        grid=(B, S // tm),
        in_specs=[
            pl.BlockSpec((1, tm, D_MODEL), lambda b, t: (b, t, 0)),
            pl.BlockSpec((1, tm, PLE_DIM), lambda b, t: (b, t, 0)),
            pl.BlockSpec((1, tm, POOL_WIDTH), lambda b, t: (b, t, 0)),
            pl.BlockSpec((1, nblk, ATTN_WIDTH, BLOCK), lambda b, t: (b, t, 0, 0)),
            pl.BlockSpec((POOL_WIDTH, D_MODEL), lambda b, t: (0, 0), pipeline_mode=pl.Buffered(1)),
            pl.BlockSpec((ATTN_WIDTH, D_MODEL), lambda b, t: (1, 0), pipeline_mode=pl.Buffered(1)),
            _resident((1, D_MODEL)),
            _resident((1, D_MODEL)),
            _resident((1, D_MODEL)),
            _resident((D_MODEL, D_FF)),
            _resident((D_FF, D_MODEL)),
            _resident((D_MODEL, D_MODEL)),
            _resident((PLE_DIM, D_MODEL)),
        ],
        out_specs=pl.BlockSpec((1, tm, D_MODEL), lambda b, t: (b, t, 0)),
        out_shape=jax.ShapeDtypeStruct((B, S, D_MODEL), f32),
        compiler_params=pltpu.CompilerParams(
            dimension_semantics=("arbitrary", "arbitrary"), vmem_limit_bytes=VMEM_LIMIT),
        name="post_mlp",
    )(x, p, ypool, oT, wo, wo, g_mix_post, g_mlp_pre, g_mlp_post, wup, wdown, wgate, wproj)
    return out


def kernel(x, p, w_in, w_pool, pool_scale, w_out, rel_bias, g_mix_pre, g_mix_post,
           g_mlp_pre, g_mlp_post, w_up, w_down, w_ple_proj, w_ple_gate):
    depth = w_in.shape[0]
    h = x
    for i in range(depth):
        h = _layer(h, p[i], w_in[i], w_pool[i], pool_scale[i:i + 1], w_out[i], rel_bias,
                   g_mix_pre[i:i + 1], g_mix_post[i:i + 1], g_mlp_pre[i:i + 1],
                   g_mlp_post[i:i + 1], w_up[i], w_down[i], w_ple_proj[i], w_ple_gate[i])
    return h
```

```python
import functools
import math

import numpy as np
import jax
import jax.numpy as jnp
from jax import lax
from jax.experimental import pallas as pl
from jax.experimental.pallas import tpu as pltpu

D_MODEL = 1024
POOL_WIDTH = 512
POOL_WINDOWS = (2, 4, 8, 16)
POOL_CH = 128
POOL_HALO = 16
ATTN_WIDTH = 512
HEAD_DIM = 64
HEADS = 8
BLOCK = 256
TOP_K = 3
NUM_BUCKETS = 32
MAX_DISTANCE = 1024
D_FF = 4096
PLE_DIM = 256
EPS = 1e-6
NEG = -1e30
LOG2E = math.log2(math.e)
KAUG = 128
NEAR = 5
CHUNK = 4
CHUNK_KEYS = CHUNK * BLOCK
ACC_ROWS = HEAD_DIM + 16
SELECT_UNROLL = 16
FINISH_UNROLL = 8
LAG_OWN = 8
LAG_PAST = 16
LAG_MAX = max(LAG_OWN, LAG_PAST)

TM_IN = 1024
TM_POST = 512
POST_PARTS = 2
VMEM_LIMIT = 56 * 1024 * 1024


def _bucket_thresholds():
    n = np.arange(0, NEAR * BLOCK + BLOCK, dtype=np.int64)
    max_exact = NUM_BUCKETS // 2
    nf = np.maximum(n, 1).astype(np.float64)
    large = max_exact + (np.log(nf / max_exact) / math.log(MAX_DISTANCE / max_exact)
                         * (NUM_BUCKETS - max_exact)).astype(np.int64)
    large = np.minimum(large, NUM_BUCKETS - 1)
    bucket = np.where(n < max_exact, n, large)
    assert np.all(np.diff(bucket) >= 0)
    return [int(np.argmax(bucket >= b)) for b in range(NUM_BUCKETS)], bucket


_THRESH, _BUCKET = _bucket_thresholds()
assert _BUCKET[(NEAR - 1) * BLOCK + 1] == NUM_BUCKETS - 1
assert NEAR <= CHUNK + 1
assert POOL_WIDTH == ATTN_WIDTH


def _rms(x, g):
    ms = jnp.mean(x * x, axis=-1, keepdims=True)
    return x * lax.rsqrt(ms + EPS) * g


def _inproj_kernel(n_cast, x_ref, g_ref, win_ref, wpool_ref, pscale_ref, *refs):
    cast_in, refs = refs[:n_cast], refs[n_cast:]
    ypool_ref, qT_ref, kaug_ref, kbar_ref, vT_ref = refs[:5]
    cast_out, (halo_scr, win_scr, wpool_scr) = refs[5:5 + n_cast], refs[5 + n_cast:]
    t = pl.program_id(1)

    @pl.when((pl.program_id(0) == 0) & (t == 0))
    def _():
        win_scr[...] = win_ref[...].astype(win_scr.dtype)
        wpool_scr[...] = wpool_ref[...].astype(wpool_scr.dtype)

    def w_cols(start, width):
        return win_scr[:, start:start + width]

    for src, dst in zip(cast_in, cast_out):
        dst[...] = src[...].astype(dst.dtype)
    tm = x_ref.shape[1]
    nblk = tm // BLOCK

    @pl.when(t == 0)
    def _():
        halo_scr[...] = jnp.zeros(halo_scr.shape, jnp.float32)

    a = _rms(x_ref[0], g_ref[...]).astype(jnp.bfloat16)

    zk = jnp.dot(a, w_cols(POOL_WIDTH + ATTN_WIDTH, ATTN_WIDTH),
                 preferred_element_type=jnp.float32)
    lane = lax.broadcasted_iota(jnp.int32, (tm, KAUG), 1)
    gblk = t * nblk + lax.broadcasted_iota(jnp.int32, (tm, KAUG), 0) // BLOCK
    onehot = jnp.where((lane == HEAD_DIM + gblk) | (lane == HEAD_DIM + 32 + gblk), 1.0, 0.0)
    for h in range(HEADS):
        pair = zk[:, (h // 2) * KAUG:(h // 2 + 1) * KAUG]
        if h % 2:
            pair = pltpu.roll(pair, HEAD_DIM, axis=1)
        kaug_ref[0, h] = jnp.where(lane < HEAD_DIM, pair, onehot).astype(kaug_ref.dtype)
    for bl in range(nblk):
        kbar_ref[0, bl] = jnp.mean(zk[bl * BLOCK:(bl + 1) * BLOCK, :], axis=0, keepdims=True)

    u = jnp.dot(a, w_cols(0, POOL_WIDTH), preferred_element_type=jnp.float32)

    zq = jnp.dot(a, w_cols(POOL_WIDTH, ATTN_WIDTH), preferred_element_type=jnp.float32)
    for bl in range(nblk):
        qT_ref[0, bl] = zq[bl * BLOCK:(bl + 1) * BLOCK, :].T

    tpos = t * tm + lax.broadcasted_iota(jnp.int32, (tm, 1), 0)
    for g, w in enumerate(POOL_WINDOWS):
        cs = slice(g * POOL_CH, (g + 1) * POOL_CH)
        ug = u[:, cs]
        acc = jnp.concatenate([halo_scr[:, cs], ug], axis=0)
        span = 1
        while span < w:
            acc = acc + pltpu.roll(acc, span, axis=0)
            span *= 2
        cnt = jnp.minimum(tpos + 1, w).astype(jnp.float32)
        d = acc[POOL_HALO:] / cnt - ug
        y = jnp.dot(d.astype(jnp.bfloat16), wpool_scr[g], preferred_element_type=jnp.float32)
        ypool_ref[0, :, cs] = (y * pscale_ref[:, cs]).astype(ypool_ref.dtype)
    halo_scr[...] = u[tm - POOL_HALO:tm, :]

    zv = jnp.dot(a, w_cols(POOL_WIDTH + 2 * ATTN_WIDTH, ATTN_WIDTH),
                 preferred_element_type=jnp.float32)
    for bl in range(nblk):
        vT_ref[0, bl] = zv[bl * BLOCK:(bl + 1) * BLOCK, :].T.astype(vT_ref.dtype)


def _bias_tile(tab_ref, h, d):
    r = lax.broadcasted_iota(jnp.int32, (BLOCK, BLOCK), 0)
    c = lax.broadcasted_iota(jnp.int32, (BLOCK, BLOCK), 1)
    n = d * BLOCK + c - r
    lo, hi = (0 if d == 0 else (d - 1) * BLOCK + 1), d * BLOCK + BLOCK - 1
    b0 = int(_BUCKET[lo])
    tile = jnp.full((BLOCK, BLOCK), tab_ref[b0, h] * LOG2E, jnp.float32)
    for b in range(b0 + 1, NUM_BUCKETS):
        if _THRESH[b] > hi:
            break
        tile = jnp.where(n >= _THRESH[b], tab_ref[b, h] * LOG2E, tile)
    if d == 0:
        tile = jnp.where(n >= 0, tile, NEG)
    return tile


def _fold8(x):
    return x.reshape(x.shape[0] // 8, 8, x.shape[1])


def _attn_items(nb):
    own = [(i, i // CHUNK) for i in range(nb)]
    past = [(i, c) for i in range(nb) for c in range(i // CHUNK)]
    return tuple(np.asarray(v, np.int32).T.copy() for v in (own, past))


def _attn_kernel(tab_ref, own_ref, past_ref, qT_ref, kaug_ref, vT_ref, kbar_ref, oT_ref,
                 qaug_scr, s_buf, p_buf, m_st, acc_st, bias_scr):
    h = pl.program_id(0)
    b = pl.program_id(1)
    nb = kbar_ref.shape[1]

    @pl.when(b == 0)
    def _():
        for d in range(bias_scr.shape[0]):
            if d < NEAR:
                bias_scr[d] = _bias_tile(tab_ref, h, d)
            else:
                bias_scr[d] = jnp.zeros((BLOCK, BLOCK), jnp.float32)

    kb2 = kbar_ref[0, :, 0, :]
    kb = jnp.where(h % 2 == 0, kb2[:, :HEAD_DIM], kb2[:, HEAD_DIM:])
    kb_hi = kb.astype(jnp.bfloat16)
    kb_lo = (kb - kb_hi.astype(jnp.float32)).astype(jnp.bfloat16)
    jidx = lax.broadcasted_iota(jnp.int32, (nb, BLOCK), 0)
    cfar = jnp.full((nb, BLOCK), tab_ref[NUM_BUCKETS - 1, h] * LOG2E, jnp.float32)
    cfar_hi = cfar.astype(jnp.bfloat16).astype(jnp.float32)
    cfar_lo = cfar - cfar_hi

    def select(i):
        qT = qT_ref[0, i]
        q_hi = qT.astype(jnp.bfloat16)
        q_lo = (qT - q_hi.astype(jnp.float32)).astype(jnp.bfloat16)
        sc = (jnp.dot(kb_hi, q_hi, preferred_element_type=jnp.float32)
              + (jnp.dot(kb_hi, q_lo, preferred_element_type=jnp.float32)
                 + jnp.dot(kb_lo, q_hi, preferred_element_type=jnp.float32)))
        past = jidx < i
        work = jnp.where(past, sc, NEG)
        chosen = jnp.zeros((nb, BLOCK), jnp.bool_)
        for _ in range(TOP_K):
            mx = jnp.max(work, axis=0, keepdims=True)
            first = jnp.min(jnp.where(work == mx, jidx, nb), axis=0, keepdims=True)
            pick = jidx == first
            chosen = chosen | pick
            work = jnp.where(pick, -jnp.inf, work)
        sel = chosen & past
        far = jidx <= i - NEAR
        code_hi = jnp.where(past, jnp.where(sel, jnp.where(far, cfar_hi, 0.0), NEG),
                            jnp.where(jidx == i, 0.0, NEG))
        code_lo = jnp.where(sel & far, cfar_lo, 0.0)
        qaug_scr[i] = jnp.concatenate(
            [(qT * (HEAD_DIM ** -0.5 * LOG2E)).astype(jnp.bfloat16),
             code_hi.astype(jnp.bfloat16), code_lo.astype(jnp.bfloat16)], axis=0)

    def select_group(g, carry):
        for u in range(SELECT_UNROLL):
            select(g * SELECT_UNROLL + u)
        return carry

    n_groups = nb // SELECT_UNROLL
    lax.fori_loop(0, n_groups - 1, select_group, 0)

    def stage_qk(item, slot, near, nblk):
        i, c = item
        q_aug = qaug_scr[i]
        cm = None
        for w in range(nblk):
            row0 = pl.multiple_of((c * CHUNK + w) * BLOCK, BLOCK)
            sw = jnp.dot(kaug_ref[0, 0, pl.ds(row0, BLOCK), :], q_aug,
                         preferred_element_type=jnp.float32)
            if near:
                sw = sw + bias_scr[jnp.minimum(i - (c * CHUNK + w), bias_scr.shape[0] - 1)]
            s_buf[slot, w] = sw
            fm = jnp.max(_fold8(sw), axis=0)
            cm = fm if cm is None else jnp.maximum(cm, fm)
        return cm

    def stage_exp(item, slot, cm, nblk, first):
        i, _ = item
        col_max = jnp.max(cm, axis=0, keepdims=True)
        if first:
            m_new = jnp.broadcast_to(col_max, cm.shape)
            alpha = None
        else:
            m_old = m_st[i]
            m_new = jnp.maximum(m_old, col_max)
            alpha = jnp.exp2(m_old - m_new)[0:1]
        m_st[i] = m_new
        m_row = m_new[0:1]
        for w in range(nblk):
            p = jnp.exp2(s_buf[slot, w] - m_row)
            p_buf[slot, w * BLOCK:(w + 1) * BLOCK, :] = p.astype(jnp.bfloat16)
        return alpha

    def ones_rows(keys):
        first = lax.broadcasted_iota(jnp.int32, (ACC_ROWS - HEAD_DIM, keys), 0) == 0
        return first.astype(jnp.bfloat16)

    def stage_pv(item, slot, alpha, nblk):
        i, c = item
        vblk = jnp.concatenate([vT_ref[0, c * CHUNK + w] for w in range(nblk)], axis=1)
        vblk = jnp.concatenate([vblk, ones_rows(nblk * BLOCK)], axis=0)
        pv = jnp.dot(vblk, p_buf[slot, :nblk * BLOCK, :],
                     preferred_element_type=jnp.float32)
        acc_st[i] = pv if alpha is None else alpha * acc_st[i] + pv

    def run(items_ref, near, lag, nblk_of_slot=lambda j: CHUNK, first=False,
            after_fill=lambda: None):
        n = items_ref.shape[1]
        assert n % lag == 0 and n >= 2 * lag and lag <= s_buf.shape[0]
        item = lambda x: (items_ref[0, x], items_ref[1, x])

        def softmax_pv(x, j, cm):
            alpha = stage_exp(item(x), j, cm, nblk_of_slot(j), first)
            stage_pv(item(x), j, alpha, nblk_of_slot(j))

        cms = tuple(stage_qk(item(x), x, near, nblk_of_slot(x)) for x in range(lag))
        after_fill()

        def body(k, cms):
            t = lag * k
            new = []
            for j in range(lag):
                softmax_pv(t - lag + j, j, cms[j])
                new.append(stage_qk(item(t + j), j, near, nblk_of_slot(j)))
            return tuple(new)

        cms = lax.fori_loop(1, n // lag, body, cms)
        for j in range(lag):
            softmax_pv(n - lag + j, j, cms[j])

    assert LAG_OWN <= (n_groups - 1) * SELECT_UNROLL
    run(own_ref, True, LAG_OWN, lambda j: j % CHUNK + 1, first=True,
        after_fill=lambda: select_group(n_groups - 1, 0))
    run(past_ref, True, LAG_PAST)

    def finish(g, carry):
        for u in range(FINISH_UNROLL):
            i = g * FINISH_UNROLL + u
            acc = acc_st[i]
            oT_ref[0, i] = (acc[:HEAD_DIM] / acc[HEAD_DIM:HEAD_DIM + 1]).astype(oT_ref.dtype)
        return carry

    lax.fori_loop(0, nb // FINISH_UNROLL, finish, 0)


def _post_kernel(x_ref, p_ref, ypool_ref, oT_ref, wop_ref, woa_ref, gpost_ref, gpre_ref,
                 gmpost_ref, wup_ref, wdown_ref, wgate_ref, wproj_ref, out_ref):
    nblk = oT_ref.shape[1]
    yat = jnp.concatenate([oT_ref[0, bl].T for bl in range(nblk)], axis=0)
    part = x_ref.shape[1] // POST_PARTS
    parts = [slice(k * part, (k + 1) * part) for k in range(POST_PARTS)]
    yat = yat.astype(jnp.bfloat16)
    mixes = [jnp.dot(ypool_ref[0, rows, :], wop_ref[...], preferred_element_type=jnp.float32)
             + jnp.dot(yat[rows], woa_ref[...], preferred_element_type=jnp.float32)
             for rows in parts]
    pe = jnp.dot(p_ref[0].astype(jnp.bfloat16), wproj_ref[...], preferred_element_type=jnp.float32)
    h1s, acts = [], []
    for rows, mix in zip(parts, mixes):
        h1 = x_ref[0, rows, :] + _rms(mix, gpost_ref[...])
        m = _rms(h1, gpre_ref[...]).astype(jnp.bfloat16)
        up = jnp.dot(m, wup_ref[...], preferred_element_type=jnp.float32)
        h1s.append(h1)
        acts.append(jnp.square(jnp.maximum(up, 0.0)).astype(jnp.bfloat16))
    fs = [jnp.dot(act, wdown_ref[...], preferred_element_type=jnp.float32) for act in acts]
    for rows, h1, f in zip(parts, h1s, fs):
        h2 = h1 + _rms(f, gmpost_ref[...])
        gate = jax.nn.sigmoid(jnp.dot(h2.astype(jnp.bfloat16), wgate_ref[...],
                                      preferred_element_type=jnp.float32))
        out_ref[0, rows, :] = h2 + gate * pe[rows]


def _resident(shape):
    return pl.BlockSpec(shape, lambda *_: (0,) * len(shape), pipeline_mode=pl.Buffered(1))


def _mixers(x, w_in, w_pool, pool_scale, rel_bias, g_mix_pre, later_weights=()):
    B, S, _ = x.shape
    nb = S // BLOCK
    bf16 = jnp.bfloat16
    f32 = jnp.float32

    tm = TM_IN
    nblk = tm // BLOCK
    steps_b = S // tm
    slab = lambda w: pl.BlockSpec((w.shape[0] // (B * steps_b), w.shape[1]),
                                  lambda b, t: (b * steps_b + t, 0))
    ypool, qT, kaug, kbar, vT, *cast = pl.pallas_call(
        functools.partial(_inproj_kernel, len(later_weights)),
        grid=(B, steps_b),
        in_specs=[
            pl.BlockSpec((1, tm, D_MODEL), lambda b, t: (b, t, 0)),
            _resident((1, D_MODEL)),
            _resident(w_in.shape),
            _resident(w_pool.shape),
            _resident((1, POOL_WIDTH)),
            *[slab(w) for w in later_weights],
        ],
        out_specs=[
            pl.BlockSpec((1, tm, POOL_WIDTH), lambda b, t: (b, t, 0)),
            pl.BlockSpec((1, nblk, ATTN_WIDTH, BLOCK), lambda b, t: (b, t, 0, 0)),
            pl.BlockSpec((1, HEADS, tm, KAUG), lambda b, t: (b, 0, t, 0)),
            pl.BlockSpec((1, nblk, 1, ATTN_WIDTH), lambda b, t: (b, t, 0, 0)),
            pl.BlockSpec((1, nblk, ATTN_WIDTH, BLOCK), lambda b, t: (b, t, 0, 0)),
            *[slab(w) for w in later_weights],
        ],
        out_shape=[
            jax.ShapeDtypeStruct((B, S, POOL_WIDTH), bf16),
            jax.ShapeDtypeStruct((B, nb, ATTN_WIDTH, BLOCK), f32),
            jax.ShapeDtypeStruct((B, HEADS, S, KAUG), bf16),
            jax.ShapeDtypeStruct((B, nb, 1, ATTN_WIDTH), f32),
            jax.ShapeDtypeStruct((B, nb, ATTN_WIDTH, BLOCK), bf16),
            *[jax.ShapeDtypeStruct(w.shape, bf16) for w in later_weights],
        ],
        scratch_shapes=[pltpu.VMEM((POOL_HALO, POOL_WIDTH), f32),
                        pltpu.VMEM(w_in.shape, bf16), pltpu.VMEM(w_pool.shape, bf16)],
        compiler_params=pltpu.CompilerParams(
            dimension_semantics=("arbitrary", "arbitrary"), vmem_limit_bytes=VMEM_LIMIT),
        name="inproj_pool",
    )(x, g_mix_pre, w_in, w_pool, pool_scale, *later_weights)

    own_items, past_items = _attn_items(nb)
    smem = pl.BlockSpec(memory_space=pltpu.SMEM)
    oT = pl.pallas_call(
        _attn_kernel,
        grid=(HEADS, B),
        in_specs=[
            smem, smem, smem,
            pl.BlockSpec((1, nb, HEAD_DIM, BLOCK), lambda h, b: (b, 0, h, 0)),
            pl.BlockSpec((1, 1, S, KAUG), lambda h, b: (b, h, 0, 0)),
            pl.BlockSpec((1, nb, HEAD_DIM, BLOCK), lambda h, b: (b, 0, h, 0)),
            pl.BlockSpec((1, nb, 1, 2 * HEAD_DIM), lambda h, b: (b, 0, 0, h // 2)),
        ],
        out_specs=pl.BlockSpec((1, nb, HEAD_DIM, BLOCK), lambda h, b: (b, 0, h, 0)),
        out_shape=jax.ShapeDtypeStruct((B, nb, ATTN_WIDTH, BLOCK), f32),
        scratch_shapes=[
            pltpu.VMEM((nb, KAUG, BLOCK), bf16),
            pltpu.VMEM((LAG_MAX, CHUNK, BLOCK, BLOCK), f32),
            pltpu.VMEM((LAG_MAX, CHUNK_KEYS, BLOCK), bf16),
            pltpu.VMEM((nb, 8, BLOCK), f32),
            pltpu.VMEM((nb, ACC_ROWS, BLOCK), f32),
            pltpu.VMEM((2 * CHUNK, BLOCK, BLOCK), f32),
        ],
        compiler_params=pltpu.CompilerParams(
            dimension_semantics=("arbitrary", "arbitrary"), vmem_limit_bytes=VMEM_LIMIT),
        name="moba_attention",
    )(rel_bias, jnp.asarray(own_items), jnp.asarray(past_items), qT, kaug, vT, kbar)
    return ypool, oT, cast


def _layer(x, p, w_in, w_pool, pool_scale, w_out, rel_bias, g_mix_pre, g_mix_post,
           g_mlp_pre, g_mlp_post, w_up, w_down, w_ple_proj, w_ple_gate):
    B, S, _ = x.shape
    bf16 = jnp.bfloat16
    f32 = jnp.float32
    ypool, oT, (wo, wup, wdown, wgate, wproj) = _mixers(
        x, w_in, w_pool, pool_scale, rel_bias, g_mix_pre,
        later_weights=(w_out, w_up, w_down, w_ple_gate, w_ple_proj))

    tm = TM_POST
    nblk = tm // BLOCK
    out = pl.pallas_call(
        _post_kernel,
        grid=(B, S // tm),
        in_specs=[
            pl.BlockSpec((1, tm, D_MODEL), lambda b, t: (b, t, 0)),
            pl.BlockSpec((1, tm, PLE_DIM), lambda b, t: (b, t, 0)),
            pl.BlockSpec((1, tm, POOL_WIDTH), lambda b, t: (b, t, 0)),
            pl.BlockSpec((1, nblk, ATTN_WIDTH, BLOCK), lambda b, t: (b, t, 0, 0)),
            pl.BlockSpec((POOL_WIDTH, D_MODEL), lambda b, t: (0, 0), pipeline_mode=pl.Buffered(1)),
            pl.BlockSpec((ATTN_WIDTH, D_MODEL), lambda b, t: (1, 0), pipeline_mode=pl.Buffered(1)),
            _resident((1, D_MODEL)),
            _resident((1, D_MODEL)),
            _resident((1, D_MODEL)),
            _resident((D_MODEL, D_FF)),
            _resident((D_FF, D_MODEL)),
            _resident((D_MODEL, D_MODEL)),
            _resident((PLE_DIM, D_MODEL)),
        ],
        out_specs=pl.BlockSpec((1, tm, D_MODEL), lambda b, t: (b, t, 0)),
        out_shape=jax.ShapeDtypeStruct((B, S, D_MODEL), f32),
        compiler_params=pltpu.CompilerParams(
            dimension_semantics=("arbitrary", "arbitrary"), vmem_limit_bytes=VMEM_LIMIT),
        name="post_mlp",
    )(x, p, ypool, oT, wo, wo, g_mix_post, g_mlp_pre, g_mlp_post, wup, wdown, wgate, wproj)
    return out


def kernel(x, p, w_in, w_pool, pool_scale, w_out, rel_bias, g_mix_pre, g_mix_post,
           g_mlp_pre, g_mlp_post, w_up, w_down, w_ple_proj, w_ple_gate):
    depth = w_in.shape[0]
    h = x
    for i in range(depth):
        h = _layer(h, p[i], w_in[i], w_pool[i], pool_scale[i:i + 1], w_out[i], rel_bias,
                   g_mix_pre[i:i + 1], g_mix_post[i:i + 1], g_mlp_pre[i:i + 1],
                   g_mlp_post[i:i + 1], w_up[i], w_down[i], w_ple_proj[i], w_ple_gate[i])
    return h
```
